```python
import math
import jax
import jax.numpy as jnp
from jax import lax
import numpy as np

D_MODEL = 1024
BATCH = 1
SEQ = 16384
DEPTH = 2
DEC_BATCH = 32
DEC_SEQ = 8
PAST_LEN = 16384
PAGE_SIZE = 128

H_A = 8
DH_A = D_MODEL // H_A
W_A = H_A * DH_A
MOBA_BLOCK = 256
MOBA_TOPK = 3
MOBA_QCHUNK = 64
POOL_WINDOWS = (2, 4, 8, 16)
N_POOL_GROUPS = len(POOL_WINDOWS)
POOL_GROUP_DIM = D_MODEL // 8
W_B = N_POOL_GROUPS * POOL_GROUP_DIM
POOL_BUF = max(POOL_WINDOWS) - 1
W_AB = W_A + W_B
SPLIT_AB = [W_A, 2 * W_A, 3 * W_A, 3 * W_A + W_B]
PROJ_AB = 3 * W_A + W_B + W_AB
H_C = 8
DH_C = D_MODEL // (2 * H_C)
DV_C = 2 * DH_C
W_C = H_C * DV_C
SPLIT_C = [W_C, 2 * W_C, 3 * W_C]
PROJ_C = 4 * W_C
DIFF_QBLOCK = 128
LAMBDA_INIT = 0.8 - 0.6 * math.exp(-0.3 * 1)
ROPE_THETA = 500000.0
ROPE_FRACTION = 4
RMS_EPS = 1e-6
SUBLN_EPS = 1e-5
F32 = jnp.float32

kernel_name = 'moba_pool_diffattn_hybrid_step'


def _rms_norm(x, g, eps=RMS_EPS):
    xf = x.astype(F32)
    y = xf * lax.rsqrt(jnp.mean(xf * xf, axis=-1, keepdims=True) + eps)
    return (y * g.astype(F32)).astype(x.dtype)


def _rope_partial(x, pos):
    d = x.shape[-1]
    rot = d // ROPE_FRACTION
    half = rot // 2
    inv = ROPE_THETA ** (-(jnp.arange(half, dtype=F32) * 2.0 / rot))
    ang = pos.astype(F32)[:, None] * inv[None, :]
    shape = (pos.shape[0],) + (1,) * (x.ndim - 3) + (half,)
    cos = jnp.cos(ang).reshape(shape)
    sin = jnp.sin(ang).reshape(shape)
    x1 = x[..., :half].astype(F32)
    x2 = x[..., half:rot].astype(F32)
    r = jnp.concatenate([x1 * cos - x2 * sin, x2 * cos + x1 * sin], axis=-1).astype(x.dtype)
    return jnp.concatenate([r, x[..., rot:]], axis=-1)


def _merge_out(x, parts, gate, w_out):
    h = jnp.concatenate(parts, axis=-1) * jax.nn.silu(gate)
    return x + h @ w_out


def _moba_attend(q, sel_k, sel_v, sel_valid, own_k, own_v, own_mask):
    s_sel = jnp.einsum('...qd,...qkd->...qk', q, sel_k).astype(F32)
    s_own = jnp.einsum('...qd,...kd->...qk', q, own_k).astype(F32)
    s = jnp.concatenate([jnp.where(sel_valid, s_sel, -jnp.inf),
                         jnp.where(own_mask, s_own, -jnp.inf)], axis=-1)
    p = jax.nn.softmax(s, axis=-1).astype(q.dtype)
    n = sel_k.shape[-2]
    return (jnp.einsum('...qk,...qkd->...qd', p[..., :n], sel_v)
            + jnp.einsum('...qk,...kd->...qd', p[..., n:], own_v))


def _moba_prompt(q, k, v):
    b, t, h, d = q.shape
    nb = -(-t // MOBA_BLOCK)
    tp = nb * MOBA_BLOCK
    pad = ((0, 0), (0, tp - t), (0, 0), (0, 0))
    qp = jnp.pad(q, pad)
    kp = jnp.pad(k, pad)
    vp = jnp.pad(v, pad)
    kb = kp.reshape(b, nb, MOBA_BLOCK, h, d)
    vb = vp.reshape(b, nb, MOBA_BLOCK, h, d)
    means = kb.astype(F32).mean(axis=2)
    ncand = max(nb, MOBA_TOPK)
    means = jnp.pad(means, ((0, 0), (0, ncand - nb), (0, 0), (0, 0)))
    qblk = jnp.arange(tp) // MOBA_BLOCK
    gate = jnp.einsum('bthd,bnhd->bhtn', qp.astype(F32), means)
    gate = jnp.where(jnp.arange(ncand)[None, :] < qblk[:, None], gate, -jnp.inf)
    _, idx = lax.top_k(gate, MOBA_TOPK)
    idx = jnp.minimum(idx, nb - 1)
    valid = jnp.arange(MOBA_TOPK)[None, :] < qblk[:, None]
    nc = tp // MOBA_QCHUNK
    qs = (qp * DH_A ** -0.5).reshape(b, nc, MOBA_QCHUNK, h, d).transpose(1, 0, 3, 2, 4)
    ids = idx.reshape(b, h, nc, MOBA_QCHUNK, MOBA_TOPK).transpose(2, 0, 1, 3, 4)
    vals = valid.reshape(nc, MOBA_QCHUNK, MOBA_TOPK)
    bi = jnp.arange(b)[:, None, None, None]
    hi = jnp.arange(h)[None, :, None, None]

    def step(args):
        c, qc, ic, vc = args
        sk = kb[bi, ic, :, hi].reshape(b, h, MOBA_QCHUNK, MOBA_TOPK * MOBA_BLOCK, d)
        sv = vb[bi, ic, :, hi].reshape(b, h, MOBA_QCHUNK, MOBA_TOPK * MOBA_BLOCK, d)
        sel_valid = jnp.repeat(vc, MOBA_BLOCK, axis=-1)
        q0 = c * MOBA_QCHUNK
        start = (q0 // MOBA_BLOCK) * MOBA_BLOCK
        ok = lax.dynamic_slice_in_dim(kp, start, MOBA_BLOCK, axis=1).transpose(0, 2, 1, 3)
        ov = lax.dynamic_slice_in_dim(vp, start, MOBA_BLOCK, axis=1).transpose(0, 2, 1, 3)
        own_mask = (start + jnp.arange(MOBA_BLOCK))[None, :] <= (q0 + jnp.arange(MOBA_QCHUNK))[:, None]
        return _moba_attend(qc, sk, sv, sel_valid, ok, ov, own_mask)

    o = lax.map(step, (jnp.arange(nc), qs, ids, vals))
    return o.transpose(1, 0, 3, 2, 4).reshape(b, tp, h, d)[:, :t]


def _moba_sample(q, k, v, cache_k, cache_v, page_table):
    b, nq, h, d = q.shape
    n_pages = page_table.shape[1]
    past = n_pages * PAGE_SIZE
    ppb = MOBA_BLOCK // PAGE_SIZE
    n_full = past // MOBA_BLOCK
    own_start = n_full * MOBA_BLOCK
    n_own_past = past - own_start
    page_means = cache_k.astype(F32).mean(axis=1)
    means = page_means[page_table[:, :n_full * ppb]].reshape(b, n_full, ppb, h, d).mean(axis=2)
    ncand = max(n_full, MOBA_TOPK)
    means = jnp.pad(means, ((0, 0), (0, ncand - n_full), (0, 0), (0, 0)))
    gate = jnp.einsum('bqhd,bnhd->bhqn', q.astype(F32), means)
    gate = jnp.where(jnp.arange(ncand) < n_full, gate, -jnp.inf)
    _, idx = lax.top_k(gate, MOBA_TOPK)
    idx = jnp.minimum(idx, max(n_full - 1, 0))
    sel_valid = jnp.repeat(jnp.arange(MOBA_TOPK) < n_full, MOBA_BLOCK)
    own_pages = page_table[:, own_start // PAGE_SIZE:]
    j = jnp.arange(n_own_past + nq)
    qi = jnp.arange(nq)
    own_mask = (j[None, :] < n_own_past) | (j[None, :] - n_own_past <= qi[:, None])
    hi = jnp.arange(h)[:, None, None, None]

    def per_seq(args):
        pt, qb, kn, vn, ib, op = args
        phys = pt[ib[..., None] * ppb + jnp.arange(ppb)]
        sk = cache_k[phys, :, hi].reshape(h, nq, MOBA_TOPK * MOBA_BLOCK, d).astype(qb.dtype)
        sv = cache_v[phys, :, hi].reshape(h, nq, MOBA_TOPK * MOBA_BLOCK, d).astype(qb.dtype)
        ok = jnp.concatenate([cache_k[op].reshape(n_own_past, h, d).astype(kn.dtype), kn], axis=0)
        ov = jnp.concatenate([cache_v[op].reshape(n_own_past, h, d).astype(vn.dtype), vn], axis=0)
        qh = (qb * DH_A ** -0.5).transpose(1, 0, 2)
        return _moba_attend(qh, sk, sv, sel_valid, ok.transpose(1, 0, 2), ov.transpose(1, 0, 2), own_mask)

    o = lax.map(per_seq, (page_table, q, k, v, idx, own_pages))
    return o.transpose(0, 2, 1, 3)


def _pool_mix(u, pos, w_pool, pool_scale):
    b, t, _ = u.shape
    uf = u.astype(F32)
    csum = jnp.cumsum(uf, axis=1)
    outs = []
    for g, w in enumerate(POOL_WINDOWS):
        sl = slice(g * POOL_GROUP_DIM, (g + 1) * POOL_GROUP_DIM)
        cg = csum[..., sl]
        lagged = jnp.concatenate([jnp.zeros((b, w, POOL_GROUP_DIM), F32), cg[:, :t - w]], axis=1)
        count = jnp.minimum(pos + 1, w).astype(F32)[None, :, None]
        outs.append((cg - lagged) / count - uf[..., sl])
    dlt = jnp.stack(outs, axis=2)
    y = jnp.einsum('btgc,gce->btge', dlt, w_pool.astype(F32)).reshape(b, t, W_B)
    return (y * pool_scale.astype(F32)).astype(u.dtype)


def _diff_lambda(lq1, lk1, lq2, lk2):
    return (jnp.exp(jnp.sum(lq1.astype(F32) * lk1.astype(F32)))
            - jnp.exp(jnp.sum(lq2.astype(F32) * lk2.astype(F32))) + LAMBDA_INIT)


def _diff_attend(q, k, v, mask, lam):
    s = jnp.einsum('...qhcd,...khcd->...hcqk', q, k).astype(F32) * (DH_C ** -0.5)
    s = jnp.where(mask, s, -jnp.inf)
    p = jax.nn.softmax(s, axis=-1)
    a = p[..., 0, :, :] - lam * p[..., 1, :, :]
    return jnp.einsum('...hqk,...khe->...qhe', a.astype(v.dtype), v)


def _diff_prompt(q, k, v, lam):
    b, t = q.shape[:2]
    nq = t // DIFF_QBLOCK
    qs = q.reshape(b, nq, DIFF_QBLOCK, H_C, 2, DH_C).transpose(1, 0, 2, 3, 4, 5)
    kpos = jnp.arange(t)

    def step(args):
        c, qc = args
        qpos = c * DIFF_QBLOCK + jnp.arange(DIFF_QBLOCK)
        return _diff_attend(qc, k, v, kpos[None, :] <= qpos[:, None], lam)

    o = lax.map(step, (jnp.arange(nq), qs))
    return o.transpose(1, 0, 2, 3, 4).reshape(b, t, H_C, DV_C)


def _diff_sample(q, k, v, cache_k, cache_v, page_table, lam):
    nq = q.shape[1]
    past = page_table.shape[1] * PAGE_SIZE
    j = jnp.arange(past + nq)
    qi = jnp.arange(nq)
    mask = (j[None, :] < past) | (j[None, :] - past <= qi[:, None])

    def per_seq(args):
        pt, qb, kn, vn = args
        pk = cache_k[pt].reshape(past, H_C, 2, DH_C).astype(kn.dtype)
        pv = cache_v[pt].reshape(past, H_C, DV_C).astype(vn.dtype)
        return _diff_attend(qb, jnp.concatenate([pk, kn], axis=0), jnp.concatenate([pv, vn], axis=0), mask, lam)

    return lax.map(per_seq, (page_table, q, k, v))


def _proj_ab(x, pos, g_norm, w_in):
    b, t, _ = x.shape
    q, k, v, u, gate = jnp.split(_rms_norm(x, g_norm) @ w_in, SPLIT_AB, axis=-1)
    q = _rope_partial(q.reshape(b, t, H_A, DH_A), pos)
    k = _rope_partial(k.reshape(b, t, H_A, DH_A), pos)
    return q, k, v.reshape(b, t, H_A, DH_A), u, gate


def _proj_c(x, pos, g_norm, w_in):
    b, t, _ = x.shape
    q, k, v, gate = jnp.split(_rms_norm(x, g_norm) @ w_in, SPLIT_C, axis=-1)
    q = _rope_partial(q.reshape(b, t, H_C, 2, DH_C), pos)
    k = _rope_partial(k.reshape(b, t, H_C, 2, DH_C), pos)
    return q, k, v.reshape(b, t, H_C, DV_C), gate


def _layer_ab_prompt(x, g_norm, w_in, w_pool, pool_scale, w_out):
    b, t, _ = x.shape
    pos = jnp.arange(t)
    q, k, v, u, gate = _proj_ab(x, pos, g_norm, w_in)
    a = _moba_prompt(q, k, v).reshape(b, t, W_A)
    p = _pool_mix(u, pos, w_pool, pool_scale)
    return _merge_out(x, [a, p], gate, w_out), k, v, u[:, t - POOL_BUF:]


def _layer_ab_sample(x, cache_k, cache_v, state_pool, page_table, g_norm, w_in, w_pool, pool_scale, w_out):
    b, s, _ = x.shape
    past = page_table.shape[1] * PAGE_SIZE
    pos = past + jnp.arange(s)
    q, k, v, u, gate = _proj_ab(x, pos, g_norm, w_in)
    a = _moba_sample(q, k, v, cache_k, cache_v, page_table).reshape(b, s, W_A)
    u_ext = jnp.concatenate([state_pool.astype(u.dtype), u], axis=1)
    pos_ext = past - POOL_BUF + jnp.arange(POOL_BUF + s)
    p = _pool_mix(u_ext, pos_ext, w_pool, pool_scale)[:, POOL_BUF:]
    return _merge_out(x, [a, p], gate, w_out), k, v, u_ext[:, s:]


def _layer_c_prompt(x, g_norm, w_in, lam, g_subln, w_out):
    b, t, _ = x.shape
    pos = jnp.arange(t)
    q, k, v, gate = _proj_c(x, pos, g_norm, w_in)
    o = _rms_norm(_diff_prompt(q, k, v, lam), g_subln, SUBLN_EPS) * (1.0 - LAMBDA_INIT)
    y = _merge_out(x, [o.reshape(b, t, W_C)], gate, w_out)
    return y, k.reshape(b, t, H_C, 2 * DH_C), v


def _layer_c_sample(x, cache_k, cache_v, page_table, g_norm, w_in, lam, g_subln, w_out):
    b, s, _ = x.shape
    pos = page_table.shape[1] * PAGE_SIZE + jnp.arange(s)
    q, k, v, gate = _proj_c(x, pos, g_norm, w_in)
    o = _rms_norm(_diff_sample(q, k, v, cache_k, cache_v, page_table, lam), g_subln, SUBLN_EPS) * (1.0 - LAMBDA_INIT)
    y = _merge_out(x, [o.reshape(b, s, W_C)], gate, w_out)
    return y, k.reshape(b, s, H_C, 2 * DH_C), v


def setup_inputs(seed: int = 0) -> dict:
    key = jax.random.key(seed)
    ks = jax.random.split(key, 24)
    n_pages = PAST_LEN // PAGE_SIZE
    n_used = DEC_BATCH * n_pages
    n_pool = (5 * n_used + 3) // 4

    def nrm(k, shape, scale=1.0):
        return jax.random.normal(k, shape, F32) * scale

    page_table = jax.random.permutation(ks[7], n_pool)[:n_used].reshape(DEC_BATCH, n_pages).astype(jnp.int32)
    return {
        'x_prompt': nrm(ks[0], (BATCH, SEQ, D_MODEL)),
        'x_sample': nrm(ks[1], (DEC_BATCH, DEC_SEQ, D_MODEL)),
        'cache_k_a': nrm(ks[2], (n_pool, PAGE_SIZE, H_A, DH_A)),
        'cache_v_a': nrm(ks[3], (n_pool, PAGE_SIZE, H_A, DH_A)),
        'cache_k_c': nrm(ks[4], (n_pool, PAGE_SIZE, H_C, 2 * DH_C)),
        'cache_v_c': nrm(ks[5], (n_pool, PAGE_SIZE, H_C, DV_C)),
        'state_pool': nrm(ks[6], (DEC_BATCH, POOL_BUF, W_B)),
        'page_table': page_table,
        'g_norm0': 1.0 + nrm(ks[8], (D_MODEL,), 0.1),
        'w_in0': nrm(ks[9], (D_MODEL, PROJ_AB), D_MODEL ** -0.5),
        'w_pool': nrm(ks[10], (N_POOL_GROUPS, POOL_GROUP_DIM, POOL_GROUP_DIM), POOL_GROUP_DIM ** -0.5),
        'pool_scale': 1.0 + nrm(ks[11], (W_B,), 0.1),
        'w_out0': nrm(ks[12], (W_AB, D_MODEL), W_AB ** -0.5),
        'g_norm1': 1.0 + nrm(ks[13], (D_MODEL,), 0.1),
        'w_in1': nrm(ks[14], (D_MODEL, PROJ_C), D_MODEL ** -0.5),
        'lambda_q1': nrm(ks[15], (DH_C,), 0.1),
        'lambda_k1': nrm(ks[16], (DH_C,), 0.1),
        'lambda_q2': nrm(ks[17], (DH_C,), 0.1),
        'lambda_k2': nrm(ks[18], (DH_C,), 0.1),
        'g_subln': 1.0 + nrm(ks[19], (DV_C,), 0.1),
        'w_out1': nrm(ks[20], (W_C, D_MODEL), W_C ** -0.5),
        'g_final': 1.0 + nrm(ks[21], (D_MODEL,), 0.1),
    }


def reference(x_prompt, x_sample, cache_k_a, cache_v_a, cache_k_c, cache_v_c, state_pool, page_table,
              g_norm0, w_in0, w_pool, pool_scale, w_out0, g_norm1, w_in1,
              lambda_q1, lambda_k1, lambda_q2, lambda_k2, g_subln, w_out1, g_final):
    lam = _diff_lambda(lambda_q1, lambda_k1, lambda_q2, lambda_k2)
    yp = x_prompt
    ys = x_sample
    for layer in range(DEPTH):
        if layer % 2 == 0:
            yp, k_a_p, v_a_p, pool_p = _layer_ab_prompt(yp, g_norm0, w_in0, w_pool, pool_scale, w_out0)
            ys, k_a_s, v_a_s, pool_s = _layer_ab_sample(ys, cache_k_a, cache_v_a, state_pool, page_table,
                                                        g_norm0, w_in0, w_pool, pool_scale, w_out0)
        else:
            yp, k_c_p, v_c_p = _layer_c_prompt(yp, g_norm1, w_in1, lam, g_subln, w_out1)
            ys, k_c_s, v_c_s = _layer_c_sample(ys, cache_k_c, cache_v_c, page_table,
                                               g_norm1, w_in1, lam, g_subln, w_out1)
    yp = _rms_norm(yp, g_final)
    ys = _rms_norm(ys, g_final)
    return (yp, ys, k_a_p, v_a_p, pool_p, k_c_p, v_c_p, k_a_s, v_a_s, pool_s, k_c_s, v_c_s)
```

```python
import functools
import math

import jax
import jax.numpy as jnp
from jax import lax
from jax.experimental import pallas as pl
from jax.experimental.pallas import tpu as pltpu

F32 = jnp.float32
BF16 = jnp.bfloat16

LANES = 128
PAGE_SIZE = 128
MOBA_BLOCK = 256
MOBA_TOPK = 3
PAGES_PER_BLOCK = MOBA_BLOCK // PAGE_SIZE
POOL_WINDOWS = (2, 4, 8, 16)
POOL_BUF = max(POOL_WINDOWS) - 1
POOL_HALO = 16
ROPE_THETA = 500000.0
ROPE_FRACTION = 4
RMS_EPS = 1e-6
SUBLN_EPS = 1e-5
LAMBDA_INIT = 0.8 - 0.6 * math.exp(-0.3 * 1)
VMEM_LIMIT = 56 * 1024 * 1024
NEG_INF = float("-inf")


def _params(*sem):
    return pltpu.CompilerParams(dimension_semantics=sem, vmem_limit_bytes=VMEM_LIMIT)


def _rope_tables(pos, period):
    rot = period // ROPE_FRACTION
    half = rot // 2
    inv = ROPE_THETA ** (-(jnp.arange(half, dtype=F32) * 2.0 / rot))
    ang = pos.astype(F32)[:, None] * inv[None, :]
    cos, sin = jnp.cos(ang), jnp.sin(ang)
    n = pos.shape[0]
    zero_h = jnp.zeros((n, half), F32)
    rest0 = jnp.zeros((n, period - rot), F32)
    c = jnp.concatenate([cos, cos, jnp.ones((n, period - rot), F32)], axis=1)
    s1 = jnp.concatenate([-sin, zero_h, rest0], axis=1)
    s2 = jnp.concatenate([zero_h, sin, rest0], axis=1)
    reps = LANES // period
    return tuple(jnp.tile(t, (1, reps)) for t in (c, s1, s2)), half


def _proj_body(x_ref, g_ref, w_ref, c_ref, s1_ref, s2_ref, *out_refs, plan, half):
    x = x_ref[...]
    ms = jnp.mean(x * x, axis=-1, keepdims=True)
    xn = (x * lax.rsqrt(ms + RMS_EPS) * g_ref[...]).astype(BF16)
    tm = x.shape[0]
    outs = iter(out_refs)
    off = 0
    for width, rope, want_f32, want_bf16, want_bf16_t in plan:
        y = jnp.dot(xn, w_ref[:, off:off + width], preferred_element_type=F32)
        off += width
        heads = width // LANES
        if rope:
            c, s1, s2 = c_ref[...], s1_ref[...], s2_ref[...]
            blocks = []
            for h in range(heads):
                blk = y[:, h * LANES:(h + 1) * LANES]
                blocks.append(blk * c + pltpu.roll(blk, LANES - half, 1) * s1 + pltpu.roll(blk, half, 1) * s2)
            y = jnp.concatenate(blocks, axis=1)
        if "rows" in want_f32:
            next(outs)[...] = y
        if "heads" in want_f32:
            o_ref = next(outs)
            for h in range(heads):
                o_ref[pl.ds(h, tm, stride=heads), :] = y[:, h * LANES:(h + 1) * LANES]
        if want_bf16:
            next(outs)[...] = y.astype(BF16)
        if want_bf16_t:
            next(outs)[...] = y.T.astype(BF16)


def _proj(x, g, w_bf16, tables, half, plan, tm):
    n, d = x.shape
    p = w_bf16.shape[1]
    assert n % tm == 0 and sum(seg[0] for seg in plan) == p
    out_shape, out_specs = [], []
    for width, _, want_f32, want_bf16, want_bf16_t in plan:
        if "rows" in want_f32:
            out_shape.append(jax.ShapeDtypeStruct((n, width), F32))
            out_specs.append(pl.BlockSpec((tm, width), lambda i: (i, 0)))
        if "heads" in want_f32:
            heads = width // LANES
            out_shape.append(jax.ShapeDtypeStruct((n * heads, LANES), F32))
            out_specs.append(pl.BlockSpec((tm * heads, LANES), lambda i: (i, 0)))
        if want_bf16:
            out_shape.append(jax.ShapeDtypeStruct((n, width), BF16))
            out_specs.append(pl.BlockSpec((tm, width), lambda i: (i, 0)))
        if want_bf16_t:
            out_shape.append(jax.ShapeDtypeStruct((width, n), BF16))
            out_specs.append(pl.BlockSpec((width, tm), lambda i: (0, i)))
    row = pl.BlockSpec((tm, LANES), lambda i: (i, 0))
    return pl.pallas_call(
        functools.partial(_proj_body, plan=plan, half=half),
        grid=(n // tm,),
        in_specs=[pl.BlockSpec((tm, d), lambda i: (i, 0)),
                  pl.BlockSpec((1, d), lambda i: (0, 0)),
                  pl.BlockSpec((d, p), lambda i: (0, 0)),
                  row, row, row],
        out_specs=out_specs,
        out_shape=out_shape,
        compiler_params=_params("parallel"),
        name="proj_in",
    )(x, g.reshape(1, d), w_bf16, *tables)


def _block_mean_body(k_ref, o_ref, *, heads):
    k = k_ref[...].reshape(MOBA_BLOCK, heads, LANES)
    o_ref[...] = jnp.sum(k, axis=0) * (1.0 / MOBA_BLOCK)


def _block_means(k_heads, heads):
    t = k_heads.shape[0] // heads
    nb = t // MOBA_BLOCK
    return pl.pallas_call(
        functools.partial(_block_mean_body, heads=heads),
        grid=(nb,),
        in_specs=[pl.BlockSpec((MOBA_BLOCK * heads, LANES), lambda i: (i, 0))],
        out_specs=pl.BlockSpec((heads, LANES), lambda i: (i, 0)),
        out_shape=jax.ShapeDtypeStruct((nb * heads, LANES), F32),
        compiler_params=_params("parallel"),
        name="moba_block_means",
    )(k_heads)


def _top_blocks(gate_t, n_valid, nb):
    row = lax.broadcasted_iota(jnp.int32, gate_t.shape, 0)
    g = jnp.where(row < n_valid, gate_t, NEG_INF)
    sel = jnp.zeros(gate_t.shape, F32)
    for _ in range(MOBA_TOPK):
        m = jnp.max(g, axis=0, keepdims=True)
        cand = jnp.where((g == m) & (g > NEG_INF), row, nb)
        idx = jnp.min(cand, axis=0, keepdims=True)
        pick = row == idx
        sel = jnp.where(pick, 1.0, sel)
        g = jnp.where(pick, NEG_INF, g)
    return sel


def _online_update(s, v_t, m_i, l_i, acc):
    m_new = jnp.maximum(m_i, jnp.max(s, axis=0, keepdims=True))
    p = jnp.exp(s - m_new)
    alpha = jnp.exp(m_i - m_new)
    l_new = alpha * l_i + jnp.sum(p, axis=0, keepdims=True)
    acc_new = acc * alpha + jnp.dot(v_t, p.astype(BF16), preferred_element_type=F32)
    return m_new, l_new, acc_new


_NT = (((1,), (1,)), ((), ()))


def _moba_prompt_body(q_ref, means_ref, k_ref, vt_ref, o_ref, sel_ref, *, nb, heads, scale):
    qi = pl.program_id(1)
    q = q_ref[...]
    means = means_ref[pl.ds(pl.program_id(0), nb, stride=heads), :]
    gate_t = lax.dot_general(means, q, _NT, precision=lax.Precision.HIGHEST,
                             preferred_element_type=F32)
    sel_ref[...] = _top_blocks(gate_t, qi, nb)
    qs = (q * scale).astype(BF16)
    blk = MOBA_BLOCK

    def scores(j):
        kj = k_ref[pl.ds(pl.multiple_of(j * blk, blk), blk), :]
        return lax.dot_general(kj, qs, _NT, preferred_element_type=F32)

    def values_t(j):
        return vt_ref[:, pl.ds(pl.multiple_of(j * blk, blk), blk)]

    key = lax.broadcasted_iota(jnp.int32, (blk, blk), 0)
    qry = lax.broadcasted_iota(jnp.int32, (blk, blk), 1)
    s = jnp.where(key <= qry, scores(qi), NEG_INF)
    carry = _online_update(s, values_t(qi), jnp.full((1, blk), NEG_INF, F32), jnp.zeros((1, blk), F32),
                           jnp.zeros((LANES, blk), F32))

    def step(j, carry):
        chosen = sel_ref[pl.ds(j, 1), :] > 0.0
        s = jnp.where(chosen, scores(j), NEG_INF)
        return _online_update(s, values_t(j), *carry)

    _, l_i, acc = lax.fori_loop(0, qi, step, carry)
    o_ref[...] = (acc / l_i).T


def _moba_prompt(q, means, k_bf16, vt_bf16):
    t, w = q.shape
    nb = t // MOBA_BLOCK
    heads = w // LANES
    return pl.pallas_call(
        functools.partial(_moba_prompt_body, nb=nb, heads=heads, scale=LANES ** -0.5),
        grid=(heads, nb),
        in_specs=[pl.BlockSpec((MOBA_BLOCK, LANES), lambda h, i: (i, h)),
                  pl.BlockSpec((nb * heads, LANES), lambda h, i: (0, 0)),
                  pl.BlockSpec((t, LANES), lambda h, i: (0, h)),
                  pl.BlockSpec((LANES, t), lambda h, i: (h, 0))],
        out_specs=pl.BlockSpec((MOBA_BLOCK, LANES), lambda h, i: (i, h)),
        out_shape=jax.ShapeDtypeStruct((t, w), F32),
        scratch_shapes=[pltpu.VMEM((nb, MOBA_BLOCK), F32)],
        compiler_params=_params("parallel", "arbitrary"),
        name="moba_prompt",
    )(q, means, k_bf16, vt_bf16)


def _pool_body(prev_ref, cur_ref, w_ref, scale_ref, o_ref, ext_ref, *, tm, pos0, zero_first_halo):
    i = pl.program_id(1)
    prev = prev_ref[0]
    if zero_first_halo:
        prev = jnp.where(i == 0, 0.0, prev)
    ext_ref[0:POOL_HALO, :] = prev
    ext_ref[POOL_HALO:POOL_HALO + tm, :] = cur_ref[0]
    pos = pos0 + i * tm + lax.broadcasted_iota(jnp.int32, (tm, 1), 0)
    for g, win in enumerate(POOL_WINDOWS):
        cols = slice(g * LANES, (g + 1) * LANES)
        tok = ext_ref[POOL_HALO:POOL_HALO + tm, cols]
        tot = tok
        for back in range(1, win):
            tot = tot + ext_ref[POOL_HALO - back:POOL_HALO - back + tm, cols]
        count = jnp.minimum(pos + 1, win).astype(F32)
        dlt = (tot / count - tok).astype(BF16)
        y = jnp.dot(dlt, w_ref[g], preferred_element_type=F32)
        o_ref[0, :, cols] = y * scale_ref[:, cols]


def _pool_mix(prev, cur, prev_block_of, w_pool_bf16, pool_scale, tm, pos0, zero_first_halo):
    b, t, w = cur.shape
    assert t % tm == 0
    return pl.pallas_call(
        functools.partial(_pool_body, tm=tm, pos0=pos0, zero_first_halo=zero_first_halo),
        grid=(b, t // tm),
        in_specs=[pl.BlockSpec((1, POOL_HALO, w), lambda bi, i: (bi, prev_block_of(i), 0)),
                  pl.BlockSpec((1, tm, w), lambda bi, i: (bi, i, 0)),
                  pl.BlockSpec(w_pool_bf16.shape, lambda bi, i: (0, 0, 0)),
                  pl.BlockSpec((1, w), lambda bi, i: (0, 0))],
        out_specs=pl.BlockSpec((1, tm, w), lambda bi, i: (bi, i, 0)),
        out_shape=jax.ShapeDtypeStruct((b, t, w), F32),
        scratch_shapes=[pltpu.VMEM((POOL_HALO + tm, w), F32)],
        compiler_params=_params("parallel", "parallel"),
        name="pool_mix",
    )(prev, cur, w_pool_bf16, pool_scale.reshape(1, w))


def _silu(g):
    return g / (1.0 + jnp.exp(-g))


def _merge_body(*refs, n_parts, final_norm):
    part_refs = refs[:n_parts]
    gate_ref, x_ref, w_ref, gf_ref, o_ref = refs[n_parts:]
    y = x_ref[...]
    off = 0
    for p_ref in part_refs:
        width = p_ref.shape[1]
        h = (p_ref[...] * _silu(gate_ref[:, off:off + width])).astype(BF16)
        y = y + jnp.dot(h, w_ref[off:off + width, :], preferred_element_type=F32)
        off += width
    if final_norm:
        ms = jnp.mean(y * y, axis=-1, keepdims=True)
        y = y * lax.rsqrt(ms + RMS_EPS) * gf_ref[...]
    o_ref[...] = y


def _merge_out(parts, gate, x, w_out_bf16, g_final, final_norm, tm):
    n, d = x.shape
    assert n % tm == 0
    wsum = w_out_bf16.shape[0]
    in_specs = [pl.BlockSpec((tm, p.shape[1]), lambda i: (i, 0)) for p in parts]
    in_specs += [pl.BlockSpec((tm, wsum), lambda i: (i, 0)),
                 pl.BlockSpec((tm, d), lambda i: (i, 0)),
                 pl.BlockSpec((wsum, d), lambda i: (0, 0)),
                 pl.BlockSpec((1, d), lambda i: (0, 0))]
    return pl.pallas_call(
        functools.partial(_merge_body, n_parts=len(parts), final_norm=final_norm),
        grid=(n // tm,),
        in_specs=in_specs,
        out_specs=pl.BlockSpec((tm, d), lambda i: (i, 0)),
        out_shape=jax.ShapeDtypeStruct((n, d), F32),
        compiler_params=_params("parallel"),
        name="merge_out",
    )(*parts, gate, x, w_out_bf16, g_final.reshape(1, d))


def _lambda_of(lq1_ref, lk1_ref, lq2_ref, lk2_ref):
    a = jnp.sum(lq1_ref[...] * lk1_ref[...], axis=-1, keepdims=True)
    b = jnp.sum(lq2_ref[...] * lk2_ref[...], axis=-1, keepdims=True)
    return jnp.exp(a) - jnp.exp(b) + LAMBDA_INIT


def _split_components(q, scale):
    lane = lax.broadcasted_iota(jnp.int32, q.shape, 1)
    qs = q * scale
    half = LANES // 2
    return jnp.concatenate([jnp.where(lane < half, qs, 0.0), jnp.where(lane >= half, qs, 0.0)], axis=0).astype(BF16)


def _subln(o, g_row):
    ms = jnp.mean(o * o, axis=-1, keepdims=True)
    return o * lax.rsqrt(ms + SUBLN_EPS) * g_row * (1.0 - LAMBDA_INIT)


def _diff_prompt_body(q_ref, k_ref, vt_ref, lq1_ref, lk1_ref, lq2_ref, lk2_ref, gs_ref, o_ref, *, tq, scale):
    qi = pl.program_id(1)
    qz = _split_components(q_ref[...], scale)

    def scores(j):
        kj = k_ref[pl.ds(pl.multiple_of(j * tq, tq), tq), :]
        return lax.dot_general(kj, qz, _NT, preferred_element_type=F32)

    def values_t(j):
        return vt_ref[:, pl.ds(pl.multiple_of(j * tq, tq), tq)]

    key = lax.broadcasted_iota(jnp.int32, (tq, 2 * tq), 0)
    col = lax.broadcasted_iota(jnp.int32, (tq, 2 * tq), 1)
    qry = jnp.where(col >= tq, col - tq, col)
    s = jnp.where(key <= qry, scores(qi), NEG_INF)
    carry = _online_update(s, values_t(qi), jnp.full((1, 2 * tq), NEG_INF, F32), jnp.zeros((1, 2 * tq), F32),
                           jnp.zeros((LANES, 2 * tq), F32))

    def step(j, carry):
        return _online_update(scores(j), values_t(j), *carry)

    _, l_i, acc = lax.fori_loop(0, qi, step, carry)
    a = acc / l_i
    lam = _lambda_of(lq1_ref, lk1_ref, lq2_ref, lk2_ref)
    o = (a[:, :tq] - lam * a[:, tq:]).T
    o_ref[...] = _subln(o, gs_ref[...])


def _diff_prompt(q, k_bf16, vt_bf16, lams, g_subln, tq):
    t, w = q.shape
    heads = w // LANES
    lam_spec = pl.BlockSpec((1, LANES // 2), lambda h, i: (0, 0))
    return pl.pallas_call(
        functools.partial(_diff_prompt_body, tq=tq, scale=(LANES // 2) ** -0.5),
        grid=(heads, t // tq),
        in_specs=[pl.BlockSpec((tq, LANES), lambda h, i: (i, h)),
                  pl.BlockSpec((t, LANES), lambda h, i: (0, h)),
                  pl.BlockSpec((LANES, t), lambda h, i: (h, 0)),
                  lam_spec, lam_spec, lam_spec, lam_spec,
                  pl.BlockSpec((1, LANES), lambda h, i: (0, 0))],
        out_specs=pl.BlockSpec((tq, LANES), lambda h, i: (i, h)),
        out_shape=jax.ShapeDtypeStruct((t, w), F32),
        compiler_params=_params("parallel", "arbitrary"),
        name="diff_prompt",
    )(q, k_bf16, vt_bf16, *lams, g_subln.reshape(1, LANES))


def _head_rows(page_ref, h, heads):
    return page_ref[0, pl.ds(h, PAGE_SIZE, stride=heads), :]


def _page_means_body(pt_ref, *refs, blocks, heads):
    del pt_ref
    page_refs, o_ref = refs[:-1], refs[-1]
    for b in range(blocks):
        tot = None
        for p in range(PAGES_PER_BLOCK):
            page = page_refs[b * PAGES_PER_BLOCK + p][0].reshape(PAGE_SIZE, heads, LANES)
            part = jnp.sum(page, axis=0)
            tot = part if tot is None else tot + part
        o_ref[0, b * heads:(b + 1) * heads, :] = tot * (1.0 / MOBA_BLOCK)


def _page_block_means(cache_k, page_table, heads, blocks_per_step):
    _, rows, _ = cache_k.shape
    b, n_pages = page_table.shape
    n_blocks = n_pages // PAGES_PER_BLOCK
    assert n_blocks % blocks_per_step == 0
    pages_per_step = blocks_per_step * PAGES_PER_BLOCK

    def page_spec(p):
        return pl.BlockSpec((1, rows, LANES), lambda bi, i, pt: (pt[bi, i * pages_per_step + p], 0, 0))

    return pl.pallas_call(
        functools.partial(_page_means_body, blocks=blocks_per_step, heads=heads),
        grid_spec=pltpu.PrefetchScalarGridSpec(
            num_scalar_prefetch=1,
            grid=(b, n_blocks // blocks_per_step),
            in_specs=[page_spec(p) for p in range(pages_per_step)],
            out_specs=pl.BlockSpec((1, blocks_per_step * heads, LANES), lambda bi, i, pt: (bi, i, 0)),
        ),
        out_shape=jax.ShapeDtypeStruct((b, n_blocks * heads, LANES), F32),
        compiler_params=_params("parallel", "parallel"),
        name="page_block_means",
    )(page_table, *([cache_k] * pages_per_step))


def _sample_topk_body(q_ref, means_ref, o_ref, *, heads, nb):
    nq = q_ref.shape[1]
    col = lax.broadcasted_iota(jnp.int32, (nq, nb), 1)
    lane = lax.broadcasted_iota(jnp.int32, (nq, LANES), 1)
    for h in range(heads):
        q = q_ref[0, :, h * LANES:(h + 1) * LANES]
        means = means_ref[0, pl.ds(h, nb, stride=heads), :]
        g = lax.dot_general(q, means, _NT, precision=lax.Precision.HIGHEST, preferred_element_type=F32)
        ids = jnp.full((nq, LANES), -1, jnp.int32)
        for k in range(MOBA_TOPK):
            m = jnp.max(g, axis=1, keepdims=True)
            idx = jnp.min(jnp.where(g == m, col, nb - 1), axis=1, keepdims=True)
            ids = jnp.where(lane == k, idx, ids)
            g = jnp.where(col == idx, NEG_INF, g)
        o_ref[0, h] = ids


def _sample_topk(q, means, heads):
    b, nq, w = q.shape
    nb = means.shape[1] // heads
    assert nb >= MOBA_TOPK
    return pl.pallas_call(
        functools.partial(_sample_topk_body, heads=heads, nb=nb),
        grid=(b,),
        in_specs=[pl.BlockSpec((1, nq, w), lambda bi: (bi, 0, 0)),
                  pl.BlockSpec((1, nb * heads, LANES), lambda bi: (bi, 0, 0))],
        out_specs=pl.BlockSpec((1, heads, nq, LANES), lambda bi: (bi, 0, 0, 0)),
        out_shape=jax.ShapeDtypeStruct((b, heads, nq, LANES), jnp.int32),
        compiler_params=_params("parallel"),
        name="sample_topk",
    )(q, means)


def _row_update(m_ref, l_ref, acc_ref, h, s, v):
    m_i, l_i = m_ref[h], l_ref[h]
    m_new = jnp.maximum(m_i, jnp.max(s, axis=1, keepdims=True))
    m_safe = jnp.where(m_new == NEG_INF, 0.0, m_new)
    p = jnp.exp(s - m_safe)
    alpha = jnp.exp(m_i - m_safe)
    m_ref[h] = m_new
    l_ref[h] = alpha * l_i + jnp.sum(p, axis=1, keepdims=True)
    acc_ref[h] = acc_ref[h] * alpha + jnp.dot(p.astype(BF16), v, preferred_element_type=F32)


def _init_running(step, m_ref, l_ref, acc_ref):
    @pl.when(step == 0)
    def _():
        m_ref[...] = jnp.full(m_ref.shape, NEG_INF, F32)
        l_ref[...] = jnp.zeros(l_ref.shape, F32)
        acc_ref[...] = jnp.zeros(acc_ref.shape, F32)


def _moba_sample_body(pt_ref, q_ref, kn_ref, vn_ref, ids_ref, *refs, pages, heads, scale):
    del pt_ref
    k_refs, v_refs = refs[:pages], refs[pages:2 * pages]
    o_ref, m_ref, l_ref, acc_ref = refs[2 * pages:]
    step = pl.program_id(1)
    nq = q_ref.shape[1]
    _init_running(step, m_ref, l_ref, acc_ref)

    col = lax.broadcasted_iota(jnp.int32, (nq, pages * PAGE_SIZE), 1)
    block_of_col = (step * pages * PAGE_SIZE + col) // MOBA_BLOCK
    for h in range(heads):
        cols = slice(h * LANES, (h + 1) * LANES)
        qs = (q_ref[0, :, cols] * scale).astype(BF16)
        ids = ids_ref[0, h]
        chosen = ids[:, 0:1] == block_of_col
        for k in range(1, MOBA_TOPK):
            chosen = chosen | (ids[:, k:k + 1] == block_of_col)
        kcat = jnp.concatenate([_head_rows(r, h, heads) for r in k_refs], axis=0).astype(BF16)
        vcat = jnp.concatenate([_head_rows(r, h, heads) for r in v_refs], axis=0).astype(BF16)
        s = lax.dot_general(qs, kcat, _NT, preferred_element_type=F32)
        _row_update(m_ref, l_ref, acc_ref, h, jnp.where(chosen, s, NEG_INF), vcat)

    @pl.when(step == pl.num_programs(1) - 1)
    def _():
        r = lax.broadcasted_iota(jnp.int32, (nq, nq), 0)
        tok = lax.broadcasted_iota(jnp.int32, (nq, nq), 1)
        for h in range(heads):
            cols = slice(h * LANES, (h + 1) * LANES)
            qs = (q_ref[0, :, cols] * scale).astype(BF16)
            s = lax.dot_general(qs, kn_ref[0, :, cols].astype(BF16), _NT, preferred_element_type=F32)
            _row_update(m_ref, l_ref, acc_ref, h, jnp.where(tok <= r, s, NEG_INF), vn_ref[0, :, cols].astype(BF16))
            o_ref[0, :, cols] = acc_ref[h] / l_ref[h]


def _moba_sample(q, k_new, v_new, cache_k, cache_v, ids, page_table, heads, pages_per_step):
    b, nq, w = q.shape
    rows = cache_k.shape[1]
    n_pages = page_table.shape[1]
    assert n_pages % pages_per_step == 0 and (pages_per_step * PAGE_SIZE) % MOBA_BLOCK == 0

    def page_spec(p):
        return pl.BlockSpec((1, rows, LANES), lambda bi, i, pt: (pt[bi, i * pages_per_step + p], 0, 0))

    tok_spec = pl.BlockSpec((1, nq, w), lambda bi, i, pt: (bi, 0, 0))
    return pl.pallas_call(
        functools.partial(_moba_sample_body, pages=pages_per_step, heads=heads, scale=LANES ** -0.5),
        grid_spec=pltpu.PrefetchScalarGridSpec(
            num_scalar_prefetch=1,
            grid=(b, n_pages // pages_per_step),
            in_specs=[tok_spec, tok_spec, tok_spec,
                      pl.BlockSpec((1, heads, nq, LANES), lambda bi, i, pt: (bi, 0, 0, 0))]
            + [page_spec(p) for p in range(pages_per_step)] * 2,
            out_specs=tok_spec,
            scratch_shapes=[pltpu.VMEM((heads, nq, 1), F32), pltpu.VMEM((heads, nq, 1), F32),
                            pltpu.VMEM((heads, nq, LANES), F32)],
        ),
        out_shape=jax.ShapeDtypeStruct((b, nq, w), F32),
        compiler_params=_params("parallel", "arbitrary"),
        name="moba_sample",
    )(page_table, q, k_new, v_new, ids, *([cache_k] * pages_per_step), *([cache_v] * pages_per_step))


def _diff_sample_body(pt_ref, q_ref, kn_ref, vn_ref, lq1_ref, lk1_ref, lq2_ref, lk2_ref, gs_ref, *refs,
                      pages, heads, scale):
    del pt_ref
    k_refs, v_refs = refs[:pages], refs[pages:2 * pages]
    o_ref, m_ref, l_ref, acc_ref = refs[2 * pages:]
    step = pl.program_id(1)
    nq = q_ref.shape[1]
    rows = 2 * nq
    _init_running(step, m_ref, l_ref, acc_ref)

    for h in range(heads):
        qz = _split_components(q_ref[0, :, h * LANES:(h + 1) * LANES], scale)
        kcat = jnp.concatenate([_head_rows(r, h, heads) for r in k_refs], axis=0).astype(BF16)
        vcat = jnp.concatenate([_head_rows(r, h, heads) for r in v_refs], axis=0).astype(BF16)
        _row_update(m_ref, l_ref, acc_ref, h, lax.dot_general(qz, kcat, _NT, preferred_element_type=F32), vcat)

    @pl.when(step == pl.num_programs(1) - 1)
    def _():
        lam = _lambda_of(lq1_ref, lk1_ref, lq2_ref, lk2_ref)
        r = lax.broadcasted_iota(jnp.int32, (rows, nq), 0)
        tok = lax.broadcasted_iota(jnp.int32, (rows, nq), 1)
        causal = tok <= jnp.where(r >= nq, r - nq, r)
        for h in range(heads):
            cols = slice(h * LANES, (h + 1) * LANES)
            qz = _split_components(q_ref[0, :, cols], scale)
            s = lax.dot_general(qz, kn_ref[0, :, cols].astype(BF16), _NT, preferred_element_type=F32)
            _row_update(m_ref, l_ref, acc_ref, h, jnp.where(causal, s, NEG_INF), vn_ref[0, :, cols].astype(BF16))
            a = acc_ref[h] / l_ref[h]
            o_ref[0, :, cols] = _subln(a[:nq] - lam * a[nq:], gs_ref[...])


def _diff_sample(q, k_new, v_new, cache_k, cache_v, page_table, lams, g_subln, heads, pages_per_step):
    b, nq, w = q.shape
    rows_per_page = cache_k.shape[1]
    n_pages = page_table.shape[1]
    assert n_pages % pages_per_step == 0

    def page_spec(p):
        return pl.BlockSpec((1, rows_per_page, LANES), lambda bi, i, pt: (pt[bi, i * pages_per_step + p], 0, 0))

    tok_spec = pl.BlockSpec((1, nq, w), lambda bi, i, pt: (bi, 0, 0))
    lam_spec = pl.BlockSpec((1, LANES // 2), lambda bi, i, pt: (0, 0))
    return pl.pallas_call(
        functools.partial(_diff_sample_body, pages=pages_per_step, heads=heads, scale=(LANES // 2) ** -0.5),
        grid_spec=pltpu.PrefetchScalarGridSpec(
            num_scalar_prefetch=1,
            grid=(b, n_pages // pages_per_step),
            in_specs=[tok_spec, tok_spec, tok_spec, lam_spec, lam_spec, lam_spec, lam_spec,
                      pl.BlockSpec((1, LANES), lambda bi, i, pt: (0, 0))]
            + [page_spec(p) for p in range(pages_per_step)] * 2,
            out_specs=tok_spec,
            scratch_shapes=[pltpu.VMEM((heads, 2 * nq, 1), F32), pltpu.VMEM((heads, 2 * nq, 1), F32),
                            pltpu.VMEM((heads, 2 * nq, LANES), F32)],
        ),
        out_shape=jax.ShapeDtypeStruct((b, nq, w), F32),
        compiler_params=_params("parallel", "arbitrary"),
        name="diff_sample",
    )(page_table, q, k_new, v_new, *lams, g_subln.reshape(1, LANES),
      *([cache_k] * pages_per_step), *([cache_v] * pages_per_step))


def _row_tile(n, want):
    tm = min(n, want)
    while n % tm:
        tm //= 2
    return tm


def kernel(x_prompt, x_sample, cache_k_a, cache_v_a, cache_k_c, cache_v_c, state_pool, page_table, g_norm0, w_in0, w_pool, pool_scale, w_out0, g_norm1, w_in1, lambda_q1, lambda_k1, lambda_q2, lambda_k2, g_subln, w_out1, g_final):
    bp, t, d = x_prompt.shape
    bs, nq, _ = x_sample.shape
    assert bp == 1 and t % MOBA_BLOCK == 0
    n_pool, page, heads, dh = cache_k_a.shape
    assert page == PAGE_SIZE and dh == LANES
    w_a = heads * dh
    w_b = w_pool.shape[0] * w_pool.shape[1]
    w_c = w_out1.shape[0]
    n_pages = page_table.shape[1]
    past = n_pages * PAGE_SIZE
    assert past % MOBA_BLOCK == 0 and past >= POOL_HALO

    w_in0_b, w_out0_b = w_in0.astype(BF16), w_out0.astype(BF16)
    w_in1_b, w_out1_b = w_in1.astype(BF16), w_out1.astype(BF16)
    w_pool_b = w_pool.astype(BF16)
    lams = tuple(v.reshape(1, -1) for v in (lambda_q1, lambda_k1, lambda_q2, lambda_k2))
    caches = [c.reshape(n_pool, page * heads, dh) for c in (cache_k_a, cache_v_a, cache_k_c, cache_v_c)]
    pages_per_step = 4 if n_pages % 4 == 0 else PAGES_PER_BLOCK

    pos_p = jnp.arange(t)
    pos_s = jnp.tile(past + jnp.arange(nq), bs)
    xp = x_prompt.reshape(t, d)
    xs = x_sample.reshape(bs * nq, d)
    tm_p = _row_tile(t, 512)
    tm_s = _row_tile(bs * nq, 256)
    rows, hm, both = ("rows",), ("heads",), ("rows", "heads")

    tab_p, half0 = _rope_tables(pos_p, LANES)
    tab_s, _ = _rope_tables(pos_s, LANES)
    plan0_p = ((w_a, True, rows, False, False), (w_a, True, hm, True, False), (w_a, False, hm, False, True),
               (w_b, False, rows, False, False), (w_a + w_b, False, rows, False, False))
    q0, k0, k0_b, v0, v0_t, u0, gate0 = _proj(xp, g_norm0, w_in0_b, tab_p, half0, plan0_p, tm_p)
    plan0_s = ((w_a, True, rows, False, False), (w_a, True, both, False, False), (w_a, False, both, False, False),
               (w_b, False, rows, False, False), (w_a + w_b, False, rows, False, False))
    q0s, k0s_r, k0s, v0s_r, v0s, u0s, gate0s = _proj(xs, g_norm0, w_in0_b, tab_s, half0, plan0_s, tm_s)

    a_p = _moba_prompt(q0, _block_means(k0, heads), k0_b, v0_t)
    u0_3 = u0.reshape(1, t, w_b)
    pool_tm = _row_tile(t, 512)
    p_p = _pool_mix(u0_3, u0_3, lambda i: jnp.maximum(i * (pool_tm // POOL_HALO) - 1, 0), w_pool_b, pool_scale,
                    pool_tm, 0, True)
    y0_p = _merge_out([a_p, p_p.reshape(t, w_b)], gate0, xp, w_out0_b, g_final, False, tm_p)

    n_blocks = n_pages // PAGES_PER_BLOCK
    means_s = _page_block_means(caches[0], page_table, heads, 8 if n_blocks % 8 == 0 else n_blocks)
    q0s_3, k0s_3, v0s_3 = (a.reshape(bs, nq, w_a) for a in (q0s, k0s_r, v0s_r))
    ids = _sample_topk(q0s_3, means_s, heads)
    a_s = _moba_sample(q0s_3, k0s_3, v0s_3, caches[0], caches[1], ids, page_table, heads, pages_per_step)
    u0s_3 = u0s.reshape(bs, nq, w_b)
    state_ext = jnp.concatenate([jnp.zeros((bs, POOL_HALO - POOL_BUF, w_b), F32), state_pool], axis=1)
    p_s = _pool_mix(state_ext, u0s_3, lambda i: 0, w_pool_b, pool_scale, nq, past, False)
    y0_s = _merge_out([a_s.reshape(bs * nq, w_a), p_s.reshape(bs * nq, w_b)], gate0s, xs, w_out0_b, g_final, False,
                      tm_s)
    pool_p = u0_3[:, t - POOL_BUF:]
    pool_s = jnp.concatenate([state_pool, u0s_3], axis=1)[:, nq:]

    tab_p1, half1 = _rope_tables(pos_p, LANES // 2)
    tab_s1, _ = _rope_tables(pos_s, LANES // 2)
    plan1_p = ((w_c, True, rows, False, False), (w_c, True, hm, True, False), (w_c, False, hm, False, True),
               (w_c, False, rows, False, False))
    q1, k1, k1_b, v1, v1_t, gate1 = _proj(y0_p, g_norm1, w_in1_b, tab_p1, half1, plan1_p, tm_p)
    plan1_s = ((w_c, True, rows, False, False), (w_c, True, both, False, False), (w_c, False, both, False, False),
               (w_c, False, rows, False, False))
    q1s, k1s_r, k1s, v1s_r, v1s, gate1s = _proj(y0_s, g_norm1, w_in1_b, tab_s1, half1, plan1_s, tm_s)

    o_p = _diff_prompt(q1, k1_b, v1_t, lams, g_subln, _row_tile(t, 256))
    y_p = _merge_out([o_p], gate1, y0_p, w_out1_b, g_final, True, tm_p)

    q1s_3, k1s_3, v1s_3 = (a.reshape(bs, nq, w_c) for a in (q1s, k1s_r, v1s_r))
    o_s = _diff_sample(q1s_3, k1s_3, v1s_3, caches[2], caches[3], page_table, lams, g_subln, heads, pages_per_step)
    y_s = _merge_out([o_s.reshape(bs * nq, w_c)], gate1s, y0_s, w_out1_b, g_final, True, tm_s)

    hd = (heads, dh)
    return (y_p.reshape(1, t, d), y_s.reshape(bs, nq, d),
            k0.reshape(1, t, *hd), v0.reshape(1, t, *hd), pool_p,
            k1.reshape(1, t, *hd), v1.reshape(1, t, *hd),
            k0s.reshape(bs, nq, *hd), v0s.reshape(bs, nq, *hd), pool_s,
            k1s.reshape(bs, nq, *hd), v1s.reshape(bs, nq, *hd))
```

```python
import functools
import math

import jax
import jax.numpy as jnp
from jax import lax
from jax.experimental import pallas as pl
from jax.experimental.pallas import tpu as pltpu

F32 = jnp.float32
BF16 = jnp.bfloat16

LANES = 128
PAGE_SIZE = 128
MOBA_BLOCK = 256
MOBA_TOPK = 3
PAGES_PER_BLOCK = MOBA_BLOCK // PAGE_SIZE
POOL_WINDOWS = (2, 4, 8, 16)
POOL_BUF = max(POOL_WINDOWS) - 1
POOL_HALO = 16
ROPE_THETA = 500000.0
ROPE_FRACTION = 4
RMS_EPS = 1e-6
SUBLN_EPS = 1e-5
LAMBDA_INIT = 0.8 - 0.6 * math.exp(-0.3 * 1)
VMEM_LIMIT = 56 * 1024 * 1024
NEG_INF = float("-inf")


def _params(*sem):
    return pltpu.CompilerParams(dimension_semantics=sem, vmem_limit_bytes=VMEM_LIMIT)


def _rope_tables(pos, period):
    rot = period // ROPE_FRACTION
    half = rot // 2
    inv = ROPE_THETA ** (-(jnp.arange(half, dtype=F32) * 2.0 / rot))
    ang = pos.astype(F32)[:, None] * inv[None, :]
    cos, sin = jnp.cos(ang), jnp.sin(ang)
    n = pos.shape[0]
    zero_h = jnp.zeros((n, half), F32)
    rest0 = jnp.zeros((n, period - rot), F32)
    c = jnp.concatenate([cos, cos, jnp.ones((n, period - rot), F32)], axis=1)
    s1 = jnp.concatenate([-sin, zero_h, rest0], axis=1)
    s2 = jnp.concatenate([zero_h, sin, rest0], axis=1)
    reps = LANES // period
    return tuple(jnp.tile(t, (1, reps)) for t in (c, s1, s2)), half


def _proj_body(x_ref, g_ref, w_ref, c_ref, s1_ref, s2_ref, *out_refs, plan, half):
    x = x_ref[...]
    ms = jnp.mean(x * x, axis=-1, keepdims=True)
    xn = (x * lax.rsqrt(ms + RMS_EPS) * g_ref[...]).astype(BF16)
    tm = x.shape[0]
    outs = iter(out_refs)
    off = 0
    for width, rope, want_f32, want_bf16, want_bf16_t in plan:
        y = jnp.dot(xn, w_ref[:, off:off + width], preferred_element_type=F32)
        off += width
        heads = width // LANES
        if rope:
            c, s1, s2 = c_ref[...], s1_ref[...], s2_ref[...]
            blocks = []
            for h in range(heads):
                blk = y[:, h * LANES:(h + 1) * LANES]
                blocks.append(blk * c + pltpu.roll(blk, LANES - half, 1) * s1 + pltpu.roll(blk, half, 1) * s2)
            y = jnp.concatenate(blocks, axis=1)
        if "rows" in want_f32:
            next(outs)[...] = y
        if "heads" in want_f32:
            o_ref = next(outs)
            for h in range(heads):
                o_ref[pl.ds(h, tm, stride=heads), :] = y[:, h * LANES:(h + 1) * LANES]
        if want_bf16:
            next(outs)[...] = y.astype(BF16)
        if want_bf16_t:
            next(outs)[...] = y.T.astype(BF16)


def _proj(x, g, w_bf16, tables, half, plan, tm):
    n, d = x.shape
    p = w_bf16.shape[1]
    assert n % tm == 0 and sum(seg[0] for seg in plan) == p
    out_shape, out_specs = [], []
    for width, _, want_f32, want_bf16, want_bf16_t in plan:
        if "rows" in want_f32:
            out_shape.append(jax.ShapeDtypeStruct((n, width), F32))
            out_specs.append(pl.BlockSpec((tm, width), lambda i: (i, 0)))
        if "heads" in want_f32:
            heads = width // LANES
            out_shape.append(jax.ShapeDtypeStruct((n * heads, LANES), F32))
            out_specs.append(pl.BlockSpec((tm * heads, LANES), lambda i: (i, 0)))
        if want_bf16:
            out_shape.append(jax.ShapeDtypeStruct((n, width), BF16))
            out_specs.append(pl.BlockSpec((tm, width), lambda i: (i, 0)))
        if want_bf16_t:
            out_shape.append(jax.ShapeDtypeStruct((width, n), BF16))
            out_specs.append(pl.BlockSpec((width, tm), lambda i: (0, i)))
    row = pl.BlockSpec((tm, LANES), lambda i: (i, 0))
    return pl.pallas_call(
        functools.partial(_proj_body, plan=plan, half=half),
        grid=(n // tm,),
        in_specs=[pl.BlockSpec((tm, d), lambda i: (i, 0)),
                  pl.BlockSpec((1, d), lambda i: (0, 0)),
                  pl.BlockSpec((d, p), lambda i: (0, 0)),
                  row, row, row],
        out_specs=out_specs,
        out_shape=out_shape,
        compiler_params=_params("parallel"),
        name="proj_in",
    )(x, g.reshape(1, d), w_bf16, *tables)


def _block_mean_body(k_ref, o_ref, *, heads):
    k = k_ref[...].reshape(MOBA_BLOCK, heads, LANES)
    o_ref[...] = jnp.sum(k, axis=0) * (1.0 / MOBA_BLOCK)


def _block_means(k_heads, heads):
    t = k_heads.shape[0] // heads
    nb = t // MOBA_BLOCK
    return pl.pallas_call(
        functools.partial(_block_mean_body, heads=heads),
        grid=(nb,),
        in_specs=[pl.BlockSpec((MOBA_BLOCK * heads, LANES), lambda i: (i, 0))],
        out_specs=pl.BlockSpec((heads, LANES), lambda i: (i, 0)),
        out_shape=jax.ShapeDtypeStruct((nb * heads, LANES), F32),
        compiler_params=_params("parallel"),
        name="moba_block_means",
    )(k_heads)


def _top_blocks(gate_t, n_valid, nb):
    row = lax.broadcasted_iota(jnp.int32, gate_t.shape, 0)
    g = jnp.where(row < n_valid, gate_t, NEG_INF)
    sel = jnp.zeros(gate_t.shape, F32)
    for _ in range(MOBA_TOPK):
        m = jnp.max(g, axis=0, keepdims=True)
        cand = jnp.where((g == m) & (g > NEG_INF), row, nb)
        idx = jnp.min(cand, axis=0, keepdims=True)
        pick = row == idx
        sel = jnp.where(pick, 1.0, sel)
        g = jnp.where(pick, NEG_INF, g)
    return sel


def _online_update(s, v_t, m_i, l_i, acc):
    m_new = jnp.maximum(m_i, jnp.max(s, axis=0, keepdims=True))
    p = jnp.exp2(s - m_new)
    alpha = jnp.exp2(m_i - m_new)
    l_new = alpha * l_i + jnp.sum(p, axis=0, keepdims=True)
    acc_new = acc * alpha + jnp.dot(v_t, p.astype(BF16), preferred_element_type=F32)
    return m_new, l_new, acc_new


def _online_update_guarded(s, v_t, m_i, l_i, acc):
    m_new = jnp.maximum(m_i, jnp.max(s, axis=0, keepdims=True))
    m_safe = jnp.where(m_new == NEG_INF, 0.0, m_new)
    p = jnp.exp2(s - m_safe)
    alpha = jnp.exp2(m_i - m_safe)
    l_new = alpha * l_i + jnp.sum(p, axis=0, keepdims=True)
    acc_new = acc * alpha + jnp.dot(v_t, p.astype(BF16), preferred_element_type=F32)
    return m_new, l_new, acc_new


def _fresh_state(cols):
    return jnp.full((1, cols), NEG_INF, F32), jnp.zeros((1, cols), F32), jnp.zeros((LANES, cols), F32)


_NT = (((1,), (1,)), ((), ()))
LOG2E = math.log2(math.e)
FLASH_UNROLLS = (4, 2, 1)


def _flash_tiles(scores, values_t, mask_of, n_tiles, state, s_ref):
    last = jnp.maximum(n_tiles - 1, 0)
    s_ref[0] = scores(0)
    done = 0
    for width in FLASH_UNROLLS:
        trips = (n_tiles - done) // width
        base = done

        def step(i, state, width=width, base=base):
            n = base + width * i
            for u in range(width):
                s_ref[(u + 1) % 2] = scores(jnp.minimum(n + u + 1, last))
                s = s_ref[u % 2]
                mask = mask_of(n + u)
                if mask is not None:
                    s = jnp.where(mask, s, NEG_INF)
                state = _online_update(s, values_t(n + u), *state)
            return state

        state = lax.fori_loop(0, trips, step, state)
        done = done + trips * width
    return state


def _moba_prompt_body(q_ref, means_ref, k_ref, vt_ref, o_ref, sel_ref, s_ref, *, nb, heads, scale):
    pair = pl.program_id(1)
    blk = MOBA_BLOCK
    first = 2 * pair
    q = q_ref[...]
    means = means_ref[pl.ds(pl.program_id(0), nb, stride=heads), :]
    gate_t = lax.dot_general(means, q, _NT, precision=lax.Precision.HIGHEST, preferred_element_type=F32)
    col = lax.broadcasted_iota(jnp.int32, (1, 2 * blk), 1)
    sel_ref[...] = _top_blocks(gate_t, first + (col >= blk).astype(jnp.int32), nb)
    qs = (q * (scale * LOG2E)).astype(BF16)

    def scores(j):
        kj = k_ref[pl.ds(pl.multiple_of(j * blk, blk), blk), :]
        return lax.dot_general(kj, qs, _NT, preferred_element_type=F32)

    def values_t(j):
        return vt_ref[:, pl.ds(pl.multiple_of(j * blk, blk), blk)]

    key = lax.broadcasted_iota(jnp.int32, (blk, blk), 0)
    qry = lax.broadcasted_iota(jnp.int32, (blk, blk), 1)
    causal = key <= qry
    chosen = jnp.broadcast_to(sel_ref[pl.ds(first, 1), blk:] > 0.0, (blk, blk))
    state = _online_update_guarded(jnp.where(jnp.concatenate([causal, chosen], axis=1), scores(first), NEG_INF),
                                   values_t(first), *_fresh_state(2 * blk))
    nothing = jnp.zeros((blk, blk), jnp.bool_)
    state = _online_update_guarded(jnp.where(jnp.concatenate([nothing, causal], axis=1), scores(first + 1), NEG_INF),
                                   values_t(first + 1), *state)
    _, l_i, acc = _flash_tiles(scores, values_t, lambda j: sel_ref[pl.ds(j, 1), :] > 0.0, first, state, s_ref)
    o_ref[...] = (acc / l_i).T


def _moba_prompt(q, means, k_bf16, vt_bf16):
    t, w = q.shape
    nb = t // MOBA_BLOCK
    heads = w // LANES
    assert nb % 2 == 0
    return pl.pallas_call(
        functools.partial(_moba_prompt_body, nb=nb, heads=heads, scale=LANES ** -0.5),
        grid=(heads, nb // 2),
        in_specs=[pl.BlockSpec((2 * MOBA_BLOCK, LANES), lambda h, i: (i, h)),
                  pl.BlockSpec((nb * heads, LANES), lambda h, i: (0, 0)),
                  pl.BlockSpec((t, LANES), lambda h, i: (0, h), pipeline_mode=pl.Buffered(1)),
                  pl.BlockSpec((LANES, t), lambda h, i: (h, 0), pipeline_mode=pl.Buffered(1))],
        out_specs=pl.BlockSpec((2 * MOBA_BLOCK, LANES), lambda h, i: (i, h)),
        out_shape=jax.ShapeDtypeStruct((t, w), F32),
        scratch_shapes=[pltpu.VMEM((nb, 2 * MOBA_BLOCK), F32), pltpu.VMEM((2, MOBA_BLOCK, 2 * MOBA_BLOCK), F32)],
        compiler_params=_params("parallel", "arbitrary"),
        name="moba_prompt",
    )(q, means, k_bf16, vt_bf16)


def _pool_body(prev_ref, cur_ref, w_ref, scale_ref, o_ref, ext_ref, *, tm, pos0, zero_first_halo):
    i = pl.program_id(1)
    prev = prev_ref[0]
    if zero_first_halo:
        prev = jnp.where(i == 0, 0.0, prev)
    ext_ref[0:POOL_HALO, :] = prev
    ext_ref[POOL_HALO:POOL_HALO + tm, :] = cur_ref[0]
    pos = pos0 + i * tm + lax.broadcasted_iota(jnp.int32, (tm, 1), 0)
    for g, win in enumerate(POOL_WINDOWS):
        cols = slice(g * LANES, (g + 1) * LANES)
        tok = ext_ref[POOL_HALO:POOL_HALO + tm, cols]
        tot = tok
        for back in range(1, win):
            tot = tot + ext_ref[POOL_HALO - back:POOL_HALO - back + tm, cols]
        count = jnp.minimum(pos + 1, win).astype(F32)
        dlt = (tot / count - tok).astype(BF16)
        y = jnp.dot(dlt, w_ref[g], preferred_element_type=F32)
        o_ref[0, :, cols] = y * scale_ref[:, cols]


def _pool_mix(prev, cur, prev_block_of, w_pool_bf16, pool_scale, tm, pos0, zero_first_halo):
    b, t, w = cur.shape
    assert t % tm == 0
    return pl.pallas_call(
        functools.partial(_pool_body, tm=tm, pos0=pos0, zero_first_halo=zero_first_halo),
        grid=(b, t // tm),
        in_specs=[pl.BlockSpec((1, POOL_HALO, w), lambda bi, i: (bi, prev_block_of(i), 0)),
                  pl.BlockSpec((1, tm, w), lambda bi, i: (bi, i, 0)),
                  pl.BlockSpec(w_pool_bf16.shape, lambda bi, i: (0, 0, 0)),
                  pl.BlockSpec((1, w), lambda bi, i: (0, 0))],
        out_specs=pl.BlockSpec((1, tm, w), lambda bi, i: (bi, i, 0)),
        out_shape=jax.ShapeDtypeStruct((b, t, w), F32),
        scratch_shapes=[pltpu.VMEM((POOL_HALO + tm, w), F32)],
        compiler_params=_params("parallel", "parallel"),
        name="pool_mix",
    )(prev, cur, w_pool_bf16, pool_scale.reshape(1, w))


def _silu(g):
    return g / (1.0 + jnp.exp(-g))


def _merge_body(*refs, n_parts, final_norm):
    part_refs = refs[:n_parts]
    gate_ref, x_ref, w_ref, gf_ref, o_ref = refs[n_parts:]
    y = x_ref[...]
    off = 0
    for p_ref in part_refs:
        width = p_ref.shape[1]
        h = (p_ref[...] * _silu(gate_ref[:, off:off + width])).astype(BF16)
        y = y + jnp.dot(h, w_ref[off:off + width, :], preferred_element_type=F32)
        off += width
    if final_norm:
        ms = jnp.mean(y * y, axis=-1, keepdims=True)
        y = y * lax.rsqrt(ms + RMS_EPS) * gf_ref[...]
    o_ref[...] = y


def _merge_out(parts, gate, x, w_out_bf16, g_final, final_norm, tm):
    n, d = x.shape
    assert n % tm == 0
    wsum = w_out_bf16.shape[0]
    in_specs = [pl.BlockSpec((tm, p.shape[1]), lambda i: (i, 0)) for p in parts]
    in_specs += [pl.BlockSpec((tm, wsum), lambda i: (i, 0)),
                 pl.BlockSpec((tm, d), lambda i: (i, 0)),
                 pl.BlockSpec((wsum, d), lambda i: (0, 0)),
                 pl.BlockSpec((1, d), lambda i: (0, 0))]
    return pl.pallas_call(
        functools.partial(_merge_body, n_parts=len(parts), final_norm=final_norm),
        grid=(n // tm,),
        in_specs=in_specs,
        out_specs=pl.BlockSpec((tm, d), lambda i: (i, 0)),
        out_shape=jax.ShapeDtypeStruct((n, d), F32),
        compiler_params=_params("parallel"),
        name="merge_out",
    )(*parts, gate, x, w_out_bf16, g_final.reshape(1, d))


def _lambda_of(lq1_ref, lk1_ref, lq2_ref, lk2_ref):
    a = jnp.sum(lq1_ref[...] * lk1_ref[...], axis=-1, keepdims=True)
    b = jnp.sum(lq2_ref[...] * lk2_ref[...], axis=-1, keepdims=True)
    return jnp.exp(a) - jnp.exp(b) + LAMBDA_INIT


def _split_components(q, scale):
    lane = lax.broadcasted_iota(jnp.int32, q.shape, 1)
    qs = q * scale
    half = LANES // 2
    return jnp.concatenate([jnp.where(lane < half, qs, 0.0), jnp.where(lane >= half, qs, 0.0)], axis=0).astype(BF16)


def _subln(o, g_row):
    ms = jnp.mean(o * o, axis=-1, keepdims=True)
    return o * lax.rsqrt(ms + SUBLN_EPS) * g_row * (1.0 - LAMBDA_INIT)


def _diff_prompt_body(q_ref, k_ref, vt_ref, lq1_ref, lk1_ref, lq2_ref, lk2_ref, gs_ref, o_ref, s_ref, *, tq, scale):
    qi = pl.program_id(1)
    qz = _split_components(q_ref[...], scale * LOG2E)

    def scores(j):
        kj = k_ref[pl.ds(pl.multiple_of(j * tq, tq), tq), :]
        return lax.dot_general(kj, qz, _NT, preferred_element_type=F32)

    def values_t(j):
        return vt_ref[:, pl.ds(pl.multiple_of(j * tq, tq), tq)]

    key = lax.broadcasted_iota(jnp.int32, (tq, tq), 0)
    qry = lax.broadcasted_iota(jnp.int32, (tq, tq), 1)
    causal = key <= qry
    state = _online_update(jnp.where(jnp.concatenate([causal, causal], axis=1), scores(qi), NEG_INF), values_t(qi),
                           *_fresh_state(2 * tq))
    _, l_i, acc = _flash_tiles(scores, values_t, lambda j: None, qi, state, s_ref)
    a = acc / l_i
    lam = _lambda_of(lq1_ref, lk1_ref, lq2_ref, lk2_ref)
    o = (a[:, :tq] - lam * a[:, tq:]).T
    o_ref[...] = _subln(o, gs_ref[...])


def _diff_prompt(q, k_bf16, vt_bf16, lams, g_subln, tq):
    t, w = q.shape
    heads = w // LANES
    lam_spec = pl.BlockSpec((1, LANES // 2), lambda h, i: (0, 0))
    return pl.pallas_call(
        functools.partial(_diff_prompt_body, tq=tq, scale=(LANES // 2) ** -0.5),
        grid=(heads, t // tq),
        in_specs=[pl.BlockSpec((tq, LANES), lambda h, i: (i, h)),
                  pl.BlockSpec((t, LANES), lambda h, i: (0, h), pipeline_mode=pl.Buffered(1)),
                  pl.BlockSpec((LANES, t), lambda h, i: (h, 0), pipeline_mode=pl.Buffered(1)),
                  lam_spec, lam_spec, lam_spec, lam_spec,
                  pl.BlockSpec((1, LANES), lambda h, i: (0, 0))],
        out_specs=pl.BlockSpec((tq, LANES), lambda h, i: (i, h)),
        out_shape=jax.ShapeDtypeStruct((t, w), F32),
        scratch_shapes=[pltpu.VMEM((2, tq, 2 * tq), F32)],
        compiler_params=_params("parallel", "arbitrary"),
        name="diff_prompt",
    )(q, k_bf16, vt_bf16, *lams, g_subln.reshape(1, LANES))


def _head_rows(page_ref, h, heads):
    return page_ref[0, pl.ds(h, PAGE_SIZE, stride=heads), :]


def _page_means_body(pt_ref, *refs, blocks, heads):
    del pt_ref
    page_refs, o_ref = refs[:-1], refs[-1]
    for b in range(blocks):
        tot = None
        for p in range(PAGES_PER_BLOCK):
            page = page_refs[b * PAGES_PER_BLOCK + p][0].reshape(PAGE_SIZE, heads, LANES)
            part = jnp.sum(page, axis=0)
            tot = part if tot is None else tot + part
        o_ref[0, b * heads:(b + 1) * heads, :] = tot * (1.0 / MOBA_BLOCK)


def _page_block_means(cache_k, page_table, heads, blocks_per_step):
    _, rows, _ = cache_k.shape
    b, n_pages = page_table.shape
    n_blocks = n_pages // PAGES_PER_BLOCK
    assert n_blocks % blocks_per_step == 0
    pages_per_step = blocks_per_step * PAGES_PER_BLOCK

    def page_spec(p):
        return pl.BlockSpec((1, rows, LANES), lambda bi, i, pt: (pt[bi, i * pages_per_step + p], 0, 0))

    return pl.pallas_call(
        functools.partial(_page_means_body, blocks=blocks_per_step, heads=heads),
        grid_spec=pltpu.PrefetchScalarGridSpec(
            num_scalar_prefetch=1,
            grid=(b, n_blocks // blocks_per_step),
            in_specs=[page_spec(p) for p in range(pages_per_step)],
            out_specs=pl.BlockSpec((1, blocks_per_step * heads, LANES), lambda bi, i, pt: (bi, i, 0)),
        ),
        out_shape=jax.ShapeDtypeStruct((b, n_blocks * heads, LANES), F32),
        compiler_params=_params("parallel", "parallel"),
        name="page_block_means",
    )(page_table, *([cache_k] * pages_per_step))


def _sample_topk_body(q_ref, means_ref, o_ref, *, heads, nb):
    nq = q_ref.shape[1]
    col = lax.broadcasted_iota(jnp.int32, (nq, nb), 1)
    lane = lax.broadcasted_iota(jnp.int32, (nq, LANES), 1)
    for h in range(heads):
        q = q_ref[0, :, h * LANES:(h + 1) * LANES]
        means = means_ref[0, pl.ds(h, nb, stride=heads), :]
        g = lax.dot_general(q, means, _NT, precision=lax.Precision.HIGHEST, preferred_element_type=F32)
        ids = jnp.full((nq, LANES), -1, jnp.int32)
        for k in range(MOBA_TOPK):
            m = jnp.max(g, axis=1, keepdims=True)
            idx = jnp.min(jnp.where(g == m, col, nb - 1), axis=1, keepdims=True)
            ids = jnp.where(lane == k, idx, ids)
            g = jnp.where(col == idx, NEG_INF, g)
        o_ref[0, h] = ids


def _sample_topk(q, means, heads):
    b, nq, w = q.shape
    nb = means.shape[1] // heads
    assert nb >= MOBA_TOPK
    return pl.pallas_call(
        functools.partial(_sample_topk_body, heads=heads, nb=nb),
        grid=(b,),
        in_specs=[pl.BlockSpec((1, nq, w), lambda bi: (bi, 0, 0)),
                  pl.BlockSpec((1, nb * heads, LANES), lambda bi: (bi, 0, 0))],
        out_specs=pl.BlockSpec((1, heads, nq, LANES), lambda bi: (bi, 0, 0, 0)),
        out_shape=jax.ShapeDtypeStruct((b, heads, nq, LANES), jnp.int32),
        compiler_params=_params("parallel"),
        name="sample_topk",
    )(q, means)


def _row_update(m_ref, l_ref, acc_ref, h, s, v):
    m_i, l_i = m_ref[h], l_ref[h]
    m_new = jnp.maximum(m_i, jnp.max(s, axis=1, keepdims=True))
    m_safe = jnp.where(m_new == NEG_INF, 0.0, m_new)
    p = jnp.exp(s - m_safe)
    alpha = jnp.exp(m_i - m_safe)
    m_ref[h] = m_new
    l_ref[h] = alpha * l_i + jnp.sum(p, axis=1, keepdims=True)
    acc_ref[h] = acc_ref[h] * alpha + jnp.dot(p.astype(BF16), v, preferred_element_type=F32)


def _init_running(step, m_ref, l_ref, acc_ref):
    @pl.when(step == 0)
    def _():
        m_ref[...] = jnp.full(m_ref.shape, NEG_INF, F32)
        l_ref[...] = jnp.zeros(l_ref.shape, F32)
        acc_ref[...] = jnp.zeros(acc_ref.shape, F32)


def _moba_sample_body(pt_ref, q_ref, kn_ref, vn_ref, ids_ref, *refs, pages, heads, scale):
    del pt_ref
    k_refs, v_refs = refs[:pages], refs[pages:2 * pages]
    o_ref, m_ref, l_ref, acc_ref = refs[2 * pages:]
    step = pl.program_id(1)
    nq = q_ref.shape[1]
    _init_running(step, m_ref, l_ref, acc_ref)

    col = lax.broadcasted_iota(jnp.int32, (nq, pages * PAGE_SIZE), 1)
    block_of_col = (step * pages * PAGE_SIZE + col) // MOBA_BLOCK
    for h in range(heads):
        cols = slice(h * LANES, (h + 1) * LANES)
        qs = (q_ref[0, :, cols] * scale).astype(BF16)
        ids = ids_ref[0, h]
        chosen = ids[:, 0:1] == block_of_col
        for k in range(1, MOBA_TOPK):
            chosen = chosen | (ids[:, k:k + 1] == block_of_col)
        kcat = jnp.concatenate([_head_rows(r, h, heads) for r in k_refs], axis=0).astype(BF16)
        vcat = jnp.concatenate([_head_rows(r, h, heads) for r in v_refs], axis=0).astype(BF16)
        s = lax.dot_general(qs, kcat, _NT, preferred_element_type=F32)
        _row_update(m_ref, l_ref, acc_ref, h, jnp.where(chosen, s, NEG_INF), vcat)

    @pl.when(step == pl.num_programs(1) - 1)
    def _():
        r = lax.broadcasted_iota(jnp.int32, (nq, nq), 0)
        tok = lax.broadcasted_iota(jnp.int32, (nq, nq), 1)
        for h in range(heads):
            cols = slice(h * LANES, (h + 1) * LANES)
            qs = (q_ref[0, :, cols] * scale).astype(BF16)
            s = lax.dot_general(qs, kn_ref[0, :, cols].astype(BF16), _NT, preferred_element_type=F32)
            _row_update(m_ref, l_ref, acc_ref, h, jnp.where(tok <= r, s, NEG_INF), vn_ref[0, :, cols].astype(BF16))
            o_ref[0, :, cols] = acc_ref[h] / l_ref[h]


def _moba_sample(q, k_new, v_new, cache_k, cache_v, ids, page_table, heads, pages_per_step):
    b, nq, w = q.shape
    rows = cache_k.shape[1]
    n_pages = page_table.shape[1]
    assert n_pages % pages_per_step == 0 and (pages_per_step * PAGE_SIZE) % MOBA_BLOCK == 0

    def page_spec(p):
        return pl.BlockSpec((1, rows, LANES), lambda bi, i, pt: (pt[bi, i * pages_per_step + p], 0, 0))

    tok_spec = pl.BlockSpec((1, nq, w), lambda bi, i, pt: (bi, 0, 0))
    return pl.pallas_call(
        functools.partial(_moba_sample_body, pages=pages_per_step, heads=heads, scale=LANES ** -0.5),
        grid_spec=pltpu.PrefetchScalarGridSpec(
            num_scalar_prefetch=1,
            grid=(b, n_pages // pages_per_step),
            in_specs=[tok_spec, tok_spec, tok_spec,
                      pl.BlockSpec((1, heads, nq, LANES), lambda bi, i, pt: (bi, 0, 0, 0))]
            + [page_spec(p) for p in range(pages_per_step)] * 2,
            out_specs=tok_spec,
            scratch_shapes=[pltpu.VMEM((heads, nq, 1), F32), pltpu.VMEM((heads, nq, 1), F32),
                            pltpu.VMEM((heads, nq, LANES), F32)],
        ),
        out_shape=jax.ShapeDtypeStruct((b, nq, w), F32),
        compiler_params=_params("parallel", "arbitrary"),
        name="moba_sample",
    )(page_table, q, k_new, v_new, ids, *([cache_k] * pages_per_step), *([cache_v] * pages_per_step))


def _diff_sample_body(pt_ref, q_ref, kn_ref, vn_ref, lq1_ref, lk1_ref, lq2_ref, lk2_ref, gs_ref, *refs,
                      pages, heads, scale):
    del pt_ref
    k_refs, v_refs = refs[:pages], refs[pages:2 * pages]
    o_ref, m_ref, l_ref, acc_ref = refs[2 * pages:]
    step = pl.program_id(1)
    nq = q_ref.shape[1]
    rows = 2 * nq
    _init_running(step, m_ref, l_ref, acc_ref)

    for h in range(heads):
        qz = _split_components(q_ref[0, :, h * LANES:(h + 1) * LANES], scale)
        kcat = jnp.concatenate([_head_rows(r, h, heads) for r in k_refs], axis=0).astype(BF16)
        vcat = jnp.concatenate([_head_rows(r, h, heads) for r in v_refs], axis=0).astype(BF16)
        _row_update(m_ref, l_ref, acc_ref, h, lax.dot_general(qz, kcat, _NT, preferred_element_type=F32), vcat)

    @pl.when(step == pl.num_programs(1) - 1)
    def _():
        lam = _lambda_of(lq1_ref, lk1_ref, lq2_ref, lk2_ref)
        r = lax.broadcasted_iota(jnp.int32, (rows, nq), 0)
        tok = lax.broadcasted_iota(jnp.int32, (rows, nq), 1)
        causal = tok <= jnp.where(r >= nq, r - nq, r)
        for h in range(heads):
            cols = slice(h * LANES, (h + 1) * LANES)
            qz = _split_components(q_ref[0, :, cols], scale)
            s = lax.dot_general(qz, kn_ref[0, :, cols].astype(BF16), _NT, preferred_element_type=F32)
            _row_update(m_ref, l_ref, acc_ref, h, jnp.where(causal, s, NEG_INF), vn_ref[0, :, cols].astype(BF16))
            a = acc_ref[h] / l_ref[h]
            o_ref[0, :, cols] = _subln(a[:nq] - lam * a[nq:], gs_ref[...])


def _diff_sample(q, k_new, v_new, cache_k, cache_v, page_table, lams, g_subln, heads, pages_per_step):
    b, nq, w = q.shape
    rows_per_page = cache_k.shape[1]
    n_pages = page_table.shape[1]
    assert n_pages % pages_per_step == 0

    def page_spec(p):
        return pl.BlockSpec((1, rows_per_page, LANES), lambda bi, i, pt: (pt[bi, i * pages_per_step + p], 0, 0))

    tok_spec = pl.BlockSpec((1, nq, w), lambda bi, i, pt: (bi, 0, 0))
    lam_spec = pl.BlockSpec((1, LANES // 2), lambda bi, i, pt: (0, 0))
    return pl.pallas_call(
        functools.partial(_diff_sample_body, pages=pages_per_step, heads=heads, scale=(LANES // 2) ** -0.5),
        grid_spec=pltpu.PrefetchScalarGridSpec(
            num_scalar_prefetch=1,
            grid=(b, n_pages // pages_per_step),
            in_specs=[tok_spec, tok_spec, tok_spec, lam_spec, lam_spec, lam_spec, lam_spec,
                      pl.BlockSpec((1, LANES), lambda bi, i, pt: (0, 0))]
            + [page_spec(p) for p in range(pages_per_step)] * 2,
            out_specs=tok_spec,
            scratch_shapes=[pltpu.VMEM((heads, 2 * nq, 1), F32), pltpu.VMEM((heads, 2 * nq, 1), F32),
                            pltpu.VMEM((heads, 2 * nq, LANES), F32)],
        ),
        out_shape=jax.ShapeDtypeStruct((b, nq, w), F32),
        compiler_params=_params("parallel", "arbitrary"),
        name="diff_sample",
    )(page_table, q, k_new, v_new, *lams, g_subln.reshape(1, LANES),
      *([cache_k] * pages_per_step), *([cache_v] * pages_per_step))


def _row_tile(n, want):
    tm = min(n, want)
    while n % tm:
        tm //= 2
    return tm


def kernel(x_prompt, x_sample, cache_k_a, cache_v_a, cache_k_c, cache_v_c, state_pool, page_table, g_norm0, w_in0, w_pool, pool_scale, w_out0, g_norm1, w_in1, lambda_q1, lambda_k1, lambda_q2, lambda_k2, g_subln, w_out1, g_final):
    bp, t, d = x_prompt.shape
    bs, nq, _ = x_sample.shape
    assert bp == 1 and t % MOBA_BLOCK == 0
    n_pool, page, heads, dh = cache_k_a.shape
    assert page == PAGE_SIZE and dh == LANES
    w_a = heads * dh
    w_b = w_pool.shape[0] * w_pool.shape[1]
    w_c = w_out1.shape[0]
    n_pages = page_table.shape[1]
    past = n_pages * PAGE_SIZE
    assert past % MOBA_BLOCK == 0 and past >= POOL_HALO

    w_in0_b, w_out0_b = w_in0.astype(BF16), w_out0.astype(BF16)
    w_in1_b, w_out1_b = w_in1.astype(BF16), w_out1.astype(BF16)
    w_pool_b = w_pool.astype(BF16)
    lams = tuple(v.reshape(1, -1) for v in (lambda_q1, lambda_k1, lambda_q2, lambda_k2))
    caches = [c.reshape(n_pool, page * heads, dh) for c in (cache_k_a, cache_v_a, cache_k_c, cache_v_c)]
    pages_per_step = 4 if n_pages % 4 == 0 else PAGES_PER_BLOCK

    pos_p = jnp.arange(t)
    pos_s = jnp.tile(past + jnp.arange(nq), bs)
    xp = x_prompt.reshape(t, d)
    xs = x_sample.reshape(bs * nq, d)
    tm_p = _row_tile(t, 512)
    tm_s = _row_tile(bs * nq, 256)
    rows, hm, both = ("rows",), ("heads",), ("rows", "heads")

    tab_p, half0 = _rope_tables(pos_p, LANES)
    tab_s, _ = _rope_tables(pos_s, LANES)
    plan0_p = ((w_a, True, rows, False, False), (w_a, True, hm, True, False), (w_a, False, hm, False, True),
               (w_b, False, rows, False, False), (w_a + w_b, False, rows, False, False))
    q0, k0, k0_b, v0, v0_t, u0, gate0 = _proj(xp, g_norm0, w_in0_b, tab_p, half0, plan0_p, tm_p)
    plan0_s = ((w_a, True, rows, False, False), (w_a, True, both, False, False), (w_a, False, both, False, False),
               (w_b, False, rows, False, False), (w_a + w_b, False, rows, False, False))
    q0s, k0s_r, k0s, v0s_r, v0s, u0s, gate0s = _proj(xs, g_norm0, w_in0_b, tab_s, half0, plan0_s, tm_s)

    a_p = _moba_prompt(q0, _block_means(k0, heads), k0_b, v0_t)
    u0_3 = u0.reshape(1, t, w_b)
    pool_tm = _row_tile(t, 512)
    p_p = _pool_mix(u0_3, u0_3, lambda i: jnp.maximum(i * (pool_tm // POOL_HALO) - 1, 0), w_pool_b, pool_scale,
                    pool_tm, 0, True)
    y0_p = _merge_out([a_p, p_p.reshape(t, w_b)], gate0, xp, w_out0_b, g_final, False, tm_p)

    n_blocks = n_pages // PAGES_PER_BLOCK
    means_s = _page_block_means(caches[0], page_table, heads, 8 if n_blocks % 8 == 0 else n_blocks)
    q0s_3, k0s_3, v0s_3 = (a.reshape(bs, nq, w_a) for a in (q0s, k0s_r, v0s_r))
    ids = _sample_topk(q0s_3, means_s, heads)
    a_s = _moba_sample(q0s_3, k0s_3, v0s_3, caches[0], caches[1], ids, page_table, heads, pages_per_step)
    u0s_3 = u0s.reshape(bs, nq, w_b)
    state_ext = jnp.concatenate([jnp.zeros((bs, POOL_HALO - POOL_BUF, w_b), F32), state_pool], axis=1)
    p_s = _pool_mix(state_ext, u0s_3, lambda i: 0, w_pool_b, pool_scale, nq, past, False)
    y0_s = _merge_out([a_s.reshape(bs * nq, w_a), p_s.reshape(bs * nq, w_b)], gate0s, xs, w_out0_b, g_final, False,
                      tm_s)
    pool_p = u0_3[:, t - POOL_BUF:]
    pool_s = jnp.concatenate([state_pool, u0s_3], axis=1)[:, nq:]

    tab_p1, half1 = _rope_tables(pos_p, LANES // 2)
    tab_s1, _ = _rope_tables(pos_s, LANES // 2)
    plan1_p = ((w_c, True, rows, False, False), (w_c, True, hm, True, False), (w_c, False, hm, False, True),
               (w_c, False, rows, False, False))
    q1, k1, k1_b, v1, v1_t, gate1 = _proj(y0_p, g_norm1, w_in1_b, tab_p1, half1, plan1_p, tm_p)
    plan1_s = ((w_c, True, rows, False, False), (w_c, True, both, False, False), (w_c, False, both, False, False),
               (w_c, False, rows, False, False))
    q1s, k1s_r, k1s, v1s_r, v1s, gate1s = _proj(y0_s, g_norm1, w_in1_b, tab_s1, half1, plan1_s, tm_s)

    o_p = _diff_prompt(q1, k1_b, v1_t, lams, g_subln, _row_tile(t, 256))
    y_p = _merge_out([o_p], gate1, y0_p, w_out1_b, g_final, True, tm_p)

    q1s_3, k1s_3, v1s_3 = (a.reshape(bs, nq, w_c) for a in (q1s, k1s_r, v1s_r))
    o_s = _diff_sample(q1s_3, k1s_3, v1s_3, caches[2], caches[3], page_table, lams, g_subln, heads, pages_per_step)
    y_s = _merge_out([o_s.reshape(bs * nq, w_c)], gate1s, y0_s, w_out1_b, g_final, True, tm_s)

    hd = (heads, dh)
    return (y_p.reshape(1, t, d), y_s.reshape(bs, nq, d),
            k0.reshape(1, t, *hd), v0.reshape(1, t, *hd), pool_p,
            k1.reshape(1, t, *hd), v1.reshape(1, t, *hd),
            k0s.reshape(bs, nq, *hd), v0s.reshape(bs, nq, *hd), pool_s,
            k1s.reshape(bs, nq, *hd), v1s.reshape(bs, nq, *hd))
```

```python
import functools
import math

import jax
import jax.numpy as jnp
from jax import lax
from jax.experimental import pallas as pl
from jax.experimental.pallas import tpu as pltpu

F32 = jnp.float32
BF16 = jnp.bfloat16

LANES = 128
PAGE_SIZE = 128
MOBA_BLOCK = 256
MOBA_TOPK = 3
PAGES_PER_BLOCK = MOBA_BLOCK // PAGE_SIZE
POOL_WINDOWS = (2, 4, 8, 16)
POOL_BUF = max(POOL_WINDOWS) - 1
POOL_HALO = 16
ROPE_THETA = 500000.0
ROPE_FRACTION = 4
RMS_EPS = 1e-6
SUBLN_EPS = 1e-5
LAMBDA_INIT = 0.8 - 0.6 * math.exp(-0.3 * 1)
VMEM_LIMIT = 56 * 1024 * 1024
NEG_INF = float("-inf")


def _params(*sem):
    return pltpu.CompilerParams(dimension_semantics=sem, vmem_limit_bytes=VMEM_LIMIT)


def _rope_tables(pos, period):
    rot = period // ROPE_FRACTION
    half = rot // 2
    inv = ROPE_THETA ** (-(jnp.arange(half, dtype=F32) * 2.0 / rot))
    ang = pos.astype(F32)[:, None] * inv[None, :]
    cos, sin = jnp.cos(ang), jnp.sin(ang)
    n = pos.shape[0]
    zero_h = jnp.zeros((n, half), F32)
    rest0 = jnp.zeros((n, period - rot), F32)
    c = jnp.concatenate([cos, cos, jnp.ones((n, period - rot), F32)], axis=1)
    s1 = jnp.concatenate([-sin, zero_h, rest0], axis=1)
    s2 = jnp.concatenate([zero_h, sin, rest0], axis=1)
    reps = LANES // period
    return tuple(jnp.tile(t, (1, reps)) for t in (c, s1, s2)), half


def _proj_body(x_ref, g_ref, w_ref, c_ref, s1_ref, s2_ref, *out_refs, plan, half):
    x = x_ref[...]
    ms = jnp.mean(x * x, axis=-1, keepdims=True)
    xn = (x * lax.rsqrt(ms + RMS_EPS) * g_ref[...]).astype(BF16)
    tm = x.shape[0]
    outs = iter(out_refs)
    off = 0
    for width, rope, want_f32, want_bf16, want_bf16_t in plan:
        y = jnp.dot(xn, w_ref[:, off:off + width], preferred_element_type=F32)
        off += width
        heads = width // LANES
        if rope:
            c, s1, s2 = c_ref[...], s1_ref[...], s2_ref[...]
            blocks = []
            for h in range(heads):
                blk = y[:, h * LANES:(h + 1) * LANES]
                blocks.append(blk * c + pltpu.roll(blk, LANES - half, 1) * s1 + pltpu.roll(blk, half, 1) * s2)
            y = jnp.concatenate(blocks, axis=1)
        if "rows" in want_f32:
            next(outs)[...] = y
        if "heads" in want_f32:
            o_ref = next(outs)
            for h in range(heads):
                o_ref[pl.ds(h, tm, stride=heads), :] = y[:, h * LANES:(h + 1) * LANES]
        if want_bf16:
            next(outs)[...] = y.astype(BF16)
        if want_bf16_t:
            next(outs)[...] = y.T.astype(BF16)


def _proj(x, g, w_bf16, tables, half, plan, tm):
    n, d = x.shape
    p = w_bf16.shape[1]
    assert n % tm == 0 and sum(seg[0] for seg in plan) == p
    out_shape, out_specs = [], []
    for width, _, want_f32, want_bf16, want_bf16_t in plan:
        if "rows" in want_f32:
            out_shape.append(jax.ShapeDtypeStruct((n, width), F32))
            out_specs.append(pl.BlockSpec((tm, width), lambda i: (i, 0)))
        if "heads" in want_f32:
            heads = width // LANES
            out_shape.append(jax.ShapeDtypeStruct((n * heads, LANES), F32))
            out_specs.append(pl.BlockSpec((tm * heads, LANES), lambda i: (i, 0)))
        if want_bf16:
            out_shape.append(jax.ShapeDtypeStruct((n, width), BF16))
            out_specs.append(pl.BlockSpec((tm, width), lambda i: (i, 0)))
        if want_bf16_t:
            out_shape.append(jax.ShapeDtypeStruct((width, n), BF16))
            out_specs.append(pl.BlockSpec((width, tm), lambda i: (0, i)))
    row = pl.BlockSpec((tm, LANES), lambda i: (i, 0))
    return pl.pallas_call(
        functools.partial(_proj_body, plan=plan, half=half),
        grid=(n // tm,),
        in_specs=[pl.BlockSpec((tm, d), lambda i: (i, 0)),
                  pl.BlockSpec((1, d), lambda i: (0, 0)),
                  pl.BlockSpec((d, p), lambda i: (0, 0)),
                  row, row, row],
        out_specs=out_specs,
        out_shape=out_shape,
        compiler_params=_params("parallel"),
        name="proj_in",
    )(x, g.reshape(1, d), w_bf16, *tables)


def _block_mean_body(k_ref, o_ref, *, heads):
    k = k_ref[...].reshape(MOBA_BLOCK, heads, LANES)
    o_ref[...] = jnp.sum(k, axis=0) * (1.0 / MOBA_BLOCK)


def _block_means(k_heads, heads):
    t = k_heads.shape[0] // heads
    nb = t // MOBA_BLOCK
    return pl.pallas_call(
        functools.partial(_block_mean_body, heads=heads),
        grid=(nb,),
        in_specs=[pl.BlockSpec((MOBA_BLOCK * heads, LANES), lambda i: (i, 0))],
        out_specs=pl.BlockSpec((heads, LANES), lambda i: (i, 0)),
        out_shape=jax.ShapeDtypeStruct((nb * heads, LANES), F32),
        compiler_params=_params("parallel"),
        name="moba_block_means",
    )(k_heads)


def _top_blocks(gate_t, n_valid, nb):
    row = lax.broadcasted_iota(jnp.int32, gate_t.shape, 0)
    g = jnp.where(row < n_valid, gate_t, NEG_INF)
    sel = jnp.zeros(gate_t.shape, F32)
    for _ in range(MOBA_TOPK):
        m = jnp.max(g, axis=0, keepdims=True)
        cand = jnp.where((g == m) & (g > NEG_INF), row, nb)
        idx = jnp.min(cand, axis=0, keepdims=True)
        pick = row == idx
        sel = jnp.where(pick, 1.0, sel)
        g = jnp.where(pick, NEG_INF, g)
    return sel


SUM_ROWS = 16


def _pv(v_t, p):
    v_aug = jnp.concatenate([v_t, jnp.ones((SUM_ROWS, v_t.shape[1]), BF16)], axis=0)
    return jnp.dot(v_aug, p.astype(BF16), preferred_element_type=F32)


def _online_update(s, v_t, m_i, acc):
    m_new = jnp.maximum(m_i, jnp.max(s, axis=0, keepdims=True))
    return m_new, acc * jnp.exp2(m_i - m_new) + _pv(v_t, jnp.exp2(s - m_new))


def _online_update_guarded(s, v_t, m_i, acc):
    m_new = jnp.maximum(m_i, jnp.max(s, axis=0, keepdims=True))
    m_safe = jnp.where(m_new == NEG_INF, 0.0, m_new)
    return m_new, acc * jnp.exp2(m_i - m_safe) + _pv(v_t, jnp.exp2(s - m_safe))


def _fresh_state(cols):
    return jnp.full((1, cols), NEG_INF, F32), jnp.zeros((LANES + SUM_ROWS, cols), F32)


def _normalised(acc):
    return acc[:LANES] / acc[LANES:LANES + 1]


_NT = (((1,), (1,)), ((), ()))
LOG2E = math.log2(math.e)
FLASH_UNROLLS = (8, 4, 2, 1)


def _flash_tiles(scores, values_t, mask_of, n_tiles, state, s_ref, update):
    done = 0
    for width in FLASH_UNROLLS:
        trips = (n_tiles - done) // width
        base = done

        def step(i, state, width=width, base=base):
            n = base + width * i
            for u in range(width):
                s_ref[(u + 1) % 2] = scores(n + u + 1)
                s = s_ref[u % 2]
                mask = mask_of(n + u)
                if mask is not None:
                    s = jnp.where(mask, s, NEG_INF)
                state = update(s, values_t(n + u), *state)
            return state

        state = lax.fori_loop(0, trips, step, state)
        done = done + trips * width
    return state


def _moba_prompt_body(q_ref, means_ref, k_ref, vt_ref, o_ref, sel_ref, s_ref, *, nb, heads, scale):
    pair = pl.program_id(1)
    blk = MOBA_BLOCK
    first = 2 * pair
    q = q_ref[...]
    means = means_ref[pl.ds(pl.program_id(0), nb, stride=heads), :]
    gate_t = lax.dot_general(means, q, _NT, precision=lax.Precision.HIGHEST, preferred_element_type=F32)
    col = lax.broadcasted_iota(jnp.int32, (1, 2 * blk), 1)
    sel_ref[...] = _top_blocks(gate_t, first + (col >= blk).astype(jnp.int32), nb)
    qs = (q * (scale * LOG2E)).astype(BF16)

    def scores(j):
        kj = k_ref[pl.ds(pl.multiple_of(j * blk, blk), blk), :]
        return lax.dot_general(kj, qs, _NT, preferred_element_type=F32)

    def values_t(j):
        return vt_ref[:, pl.ds(pl.multiple_of(j * blk, blk), blk)]

    key = lax.broadcasted_iota(jnp.int32, (blk, blk), 0)
    qry = lax.broadcasted_iota(jnp.int32, (blk, blk), 1)
    causal = key <= qry
    chosen = jnp.broadcast_to(sel_ref[pl.ds(first, 1), blk:] > 0.0, (blk, blk))
    nothing = jnp.zeros((blk, blk), jnp.bool_)
    s_own0 = jnp.where(jnp.concatenate([causal, chosen], axis=1), scores(first), NEG_INF)
    s_own1 = jnp.where(jnp.concatenate([nothing, causal], axis=1), scores(first + 1), NEG_INF)
    s_ref[0] = scores(0)
    state = _online_update_guarded(s_own0, values_t(first), *_fresh_state(2 * blk))
    state = _online_update_guarded(s_own1, values_t(first + 1), *state)
    _, acc = _flash_tiles(scores, values_t, lambda j: sel_ref[pl.ds(j, 1), :] > 0.0, first, state, s_ref,
                          _online_update)
    o_ref[...] = _normalised(acc).T


def _moba_prompt(q, means, k_bf16, vt_bf16):
    t, w = q.shape
    nb = t // MOBA_BLOCK
    heads = w // LANES
    assert nb % 2 == 0
    return pl.pallas_call(
        functools.partial(_moba_prompt_body, nb=nb, heads=heads, scale=LANES ** -0.5),
        grid=(heads, nb // 2),
        in_specs=[pl.BlockSpec((2 * MOBA_BLOCK, LANES), lambda h, i: (i, h)),
                  pl.BlockSpec((nb * heads, LANES), lambda h, i: (0, 0)),
                  pl.BlockSpec((t, LANES), lambda h, i: (0, h), pipeline_mode=pl.Buffered(1)),
                  pl.BlockSpec((LANES, t), lambda h, i: (h, 0), pipeline_mode=pl.Buffered(1))],
        out_specs=pl.BlockSpec((2 * MOBA_BLOCK, LANES), lambda h, i: (i, h)),
        out_shape=jax.ShapeDtypeStruct((t, w), F32),
        scratch_shapes=[pltpu.VMEM((nb, 2 * MOBA_BLOCK), F32), pltpu.VMEM((2, MOBA_BLOCK, 2 * MOBA_BLOCK), F32)],
        compiler_params=_params("parallel", "arbitrary"),
        name="moba_prompt",
    )(q, means, k_bf16, vt_bf16)


def _pool_body(prev_ref, cur_ref, w_ref, scale_ref, o_ref, ext_ref, *, tm, pos0, zero_first_halo):
    i = pl.program_id(1)
    prev = prev_ref[0]
    if zero_first_halo:
        prev = jnp.where(i == 0, 0.0, prev)
    ext_ref[0:POOL_HALO, :] = prev
    ext_ref[POOL_HALO:POOL_HALO + tm, :] = cur_ref[0]
    pos = pos0 + i * tm + lax.broadcasted_iota(jnp.int32, (tm, 1), 0)
    for g, win in enumerate(POOL_WINDOWS):
        cols = slice(g * LANES, (g + 1) * LANES)
        tok = ext_ref[POOL_HALO:POOL_HALO + tm, cols]
        tot = tok
        for back in range(1, win):
            tot = tot + ext_ref[POOL_HALO - back:POOL_HALO - back + tm, cols]
        count = jnp.minimum(pos + 1, win).astype(F32)
        dlt = (tot / count - tok).astype(BF16)
        y = jnp.dot(dlt, w_ref[g], preferred_element_type=F32)
        o_ref[0, :, cols] = y * scale_ref[:, cols]


def _pool_mix(prev, cur, prev_block_of, w_pool_bf16, pool_scale, tm, pos0, zero_first_halo):
    b, t, w = cur.shape
    assert t % tm == 0
    return pl.pallas_call(
        functools.partial(_pool_body, tm=tm, pos0=pos0, zero_first_halo=zero_first_halo),
        grid=(b, t // tm),
        in_specs=[pl.BlockSpec((1, POOL_HALO, w), lambda bi, i: (bi, prev_block_of(i), 0)),
                  pl.BlockSpec((1, tm, w), lambda bi, i: (bi, i, 0)),
                  pl.BlockSpec(w_pool_bf16.shape, lambda bi, i: (0, 0, 0)),
                  pl.BlockSpec((1, w), lambda bi, i: (0, 0))],
        out_specs=pl.BlockSpec((1, tm, w), lambda bi, i: (bi, i, 0)),
        out_shape=jax.ShapeDtypeStruct((b, t, w), F32),
        scratch_shapes=[pltpu.VMEM((POOL_HALO + tm, w), F32)],
        compiler_params=_params("parallel", "parallel"),
        name="pool_mix",
    )(prev, cur, w_pool_bf16, pool_scale.reshape(1, w))


def _silu(g):
    return g / (1.0 + jnp.exp(-g))


def _merge_body(*refs, n_parts, final_norm):
    part_refs = refs[:n_parts]
    gate_ref, x_ref, w_ref, gf_ref, o_ref = refs[n_parts:]
    y = x_ref[...]
    off = 0
    for p_ref in part_refs:
        width = p_ref.shape[1]
        h = (p_ref[...] * _silu(gate_ref[:, off:off + width])).astype(BF16)
        y = y + jnp.dot(h, w_ref[off:off + width, :], preferred_element_type=F32)
        off += width
    if final_norm:
        ms = jnp.mean(y * y, axis=-1, keepdims=True)
        y = y * lax.rsqrt(ms + RMS_EPS) * gf_ref[...]
    o_ref[...] = y


def _merge_out(parts, gate, x, w_out_bf16, g_final, final_norm, tm):
    n, d = x.shape
    assert n % tm == 0
    wsum = w_out_bf16.shape[0]
    in_specs = [pl.BlockSpec((tm, p.shape[1]), lambda i: (i, 0)) for p in parts]
    in_specs += [pl.BlockSpec((tm, wsum), lambda i: (i, 0)),
                 pl.BlockSpec((tm, d), lambda i: (i, 0)),
                 pl.BlockSpec((wsum, d), lambda i: (0, 0)),
                 pl.BlockSpec((1, d), lambda i: (0, 0))]
    return pl.pallas_call(
        functools.partial(_merge_body, n_parts=len(parts), final_norm=final_norm),
        grid=(n // tm,),
        in_specs=in_specs,
        out_specs=pl.BlockSpec((tm, d), lambda i: (i, 0)),
        out_shape=jax.ShapeDtypeStruct((n, d), F32),
        compiler_params=_params("parallel"),
        name="merge_out",
    )(*parts, gate, x, w_out_bf16, g_final.reshape(1, d))


def _lambda_of(lq1_ref, lk1_ref, lq2_ref, lk2_ref):
    a = jnp.sum(lq1_ref[...] * lk1_ref[...], axis=-1, keepdims=True)
    b = jnp.sum(lq2_ref[...] * lk2_ref[...], axis=-1, keepdims=True)
    return jnp.exp(a) - jnp.exp(b) + LAMBDA_INIT


def _split_components(q, scale):
    lane = lax.broadcasted_iota(jnp.int32, q.shape, 1)
    qs = q * scale
    half = LANES // 2
    return jnp.concatenate([jnp.where(lane < half, qs, 0.0), jnp.where(lane >= half, qs, 0.0)], axis=0).astype(BF16)


def _subln(o, g_row):
    ms = jnp.mean(o * o, axis=-1, keepdims=True)
    return o * lax.rsqrt(ms + SUBLN_EPS) * g_row * (1.0 - LAMBDA_INIT)


def _diff_prompt_body(q_ref, k_ref, vt_ref, lq1_ref, lk1_ref, lq2_ref, lk2_ref, gs_ref, o_ref, s_ref, *, tq, scale):
    qi = pl.program_id(1)
    qz = _split_components(q_ref[...], scale * LOG2E)

    def scores(j):
        kj = k_ref[pl.ds(pl.multiple_of(j * tq, tq), tq), :]
        return lax.dot_general(kj, qz, _NT, preferred_element_type=F32)

    def values_t(j):
        return vt_ref[:, pl.ds(pl.multiple_of(j * tq, tq), tq)]

    s_ref[0] = scores(0)
    state = _flash_tiles(scores, values_t, lambda j: None, qi, _fresh_state(2 * tq), s_ref, _online_update)
    key = lax.broadcasted_iota(jnp.int32, (tq, tq), 0)
    qry = lax.broadcasted_iota(jnp.int32, (tq, tq), 1)
    causal = key <= qry
    s_own = jnp.where(jnp.concatenate([causal, causal], axis=1), s_ref[qi % 2], NEG_INF)
    _, acc = _online_update(s_own, values_t(qi), *state)
    a = _normalised(acc)
    lam = _lambda_of(lq1_ref, lk1_ref, lq2_ref, lk2_ref)
    o = (a[:, :tq] - lam * a[:, tq:]).T
    o_ref[...] = _subln(o, gs_ref[...])


def _diff_prompt(q, k_bf16, vt_bf16, lams, g_subln, tq):
    t, w = q.shape
    heads = w // LANES
    lam_spec = pl.BlockSpec((1, LANES // 2), lambda h, i: (0, 0))
    return pl.pallas_call(
        functools.partial(_diff_prompt_body, tq=tq, scale=(LANES // 2) ** -0.5),
        grid=(heads, t // tq),
        in_specs=[pl.BlockSpec((tq, LANES), lambda h, i: (i, h)),
                  pl.BlockSpec((t, LANES), lambda h, i: (0, h), pipeline_mode=pl.Buffered(1)),
                  pl.BlockSpec((LANES, t), lambda h, i: (h, 0), pipeline_mode=pl.Buffered(1)),
                  lam_spec, lam_spec, lam_spec, lam_spec,
                  pl.BlockSpec((1, LANES), lambda h, i: (0, 0))],
        out_specs=pl.BlockSpec((tq, LANES), lambda h, i: (i, h)),
        out_shape=jax.ShapeDtypeStruct((t, w), F32),
        scratch_shapes=[pltpu.VMEM((2, tq, 2 * tq), F32)],
        compiler_params=_params("parallel", "arbitrary"),
        name="diff_prompt",
    )(q, k_bf16, vt_bf16, *lams, g_subln.reshape(1, LANES))


def _head_rows(page_ref, h, heads):
    return page_ref[0, pl.ds(h, PAGE_SIZE, stride=heads), :]


def _page_means_body(pt_ref, *refs, blocks, heads):
    del pt_ref
    page_refs, o_ref = refs[:-1], refs[-1]
    for b in range(blocks):
        tot = None
        for p in range(PAGES_PER_BLOCK):
            page = page_refs[b * PAGES_PER_BLOCK + p][0].reshape(PAGE_SIZE, heads, LANES)
            part = jnp.sum(page, axis=0)
            tot = part if tot is None else tot + part
        o_ref[0, b * heads:(b + 1) * heads, :] = tot * (1.0 / MOBA_BLOCK)


def _page_block_means(cache_k, page_table, heads, blocks_per_step):
    _, rows, _ = cache_k.shape
    b, n_pages = page_table.shape
    n_blocks = n_pages // PAGES_PER_BLOCK
    assert n_blocks % blocks_per_step == 0
    pages_per_step = blocks_per_step * PAGES_PER_BLOCK

    def page_spec(p):
        return pl.BlockSpec((1, rows, LANES), lambda bi, i, pt: (pt[bi, i * pages_per_step + p], 0, 0))

    return pl.pallas_call(
        functools.partial(_page_means_body, blocks=blocks_per_step, heads=heads),
        grid_spec=pltpu.PrefetchScalarGridSpec(
            num_scalar_prefetch=1,
            grid=(b, n_blocks // blocks_per_step),
            in_specs=[page_spec(p) for p in range(pages_per_step)],
            out_specs=pl.BlockSpec((1, blocks_per_step * heads, LANES), lambda bi, i, pt: (bi, i, 0)),
        ),
        out_shape=jax.ShapeDtypeStruct((b, n_blocks * heads, LANES), F32),
        compiler_params=_params("parallel", "parallel"),
        name="page_block_means",
    )(page_table, *([cache_k] * pages_per_step))


def _block_diag_queries(q, heads, rows_per_head):
    b = q.shape[0]
    eye = jnp.eye(heads, dtype=q.dtype)
    out = jnp.einsum("bhrd,hg->bhdgr", q, eye).reshape(b, heads * LANES, heads * rows_per_head)
    return jnp.pad(out, ((0, 0), (0, 0), (0, LANES - heads * rows_per_head)))


def _gather_heads(page_refs, heads):
    return jnp.concatenate(
        [jnp.concatenate([_head_rows(r, h, heads) for r in page_refs], axis=0).astype(BF16) for h in range(heads)],
        axis=1)


def _sample_topk_body(qbd_ref, means_ref, o_ref, *, heads, nb):
    means = jnp.concatenate([means_ref[0, pl.ds(h, nb, stride=heads), :] for h in range(heads)], axis=1)
    g = jnp.dot(means, qbd_ref[0], precision=lax.Precision.HIGHEST, preferred_element_type=F32)
    row = lax.broadcasted_iota(jnp.int32, g.shape, 0)
    slot = lax.broadcasted_iota(jnp.int32, (8, LANES), 0)
    ids = jnp.full((8, LANES), -1, jnp.int32)
    for k in range(MOBA_TOPK):
        m = jnp.max(g, axis=0, keepdims=True)
        idx = jnp.min(jnp.where(g == m, row, nb - 1), axis=0, keepdims=True)
        ids = jnp.where(slot == k, idx, ids)
        g = jnp.where(row == idx, NEG_INF, g)
    o_ref[0] = ids


def _sample_topk(qbd, means, heads):
    b = qbd.shape[0]
    nb = means.shape[1] // heads
    assert nb >= MOBA_TOPK
    return pl.pallas_call(
        functools.partial(_sample_topk_body, heads=heads, nb=nb),
        grid=(b,),
        in_specs=[pl.BlockSpec((1, heads * LANES, LANES), lambda bi: (bi, 0, 0)),
                  pl.BlockSpec((1, nb * heads, LANES), lambda bi: (bi, 0, 0))],
        out_specs=pl.BlockSpec((1, 8, LANES), lambda bi: (bi, 0, 0)),
        out_shape=jax.ShapeDtypeStruct((b, 8, LANES), jnp.int32),
        compiler_params=_params("parallel"),
        name="sample_topk",
    )(qbd, means)


def _partial_attention(s, v_cat, heads, rows_per_head):
    t = s.shape[0]
    m = jnp.max(s, axis=0, keepdims=True)
    m_safe = jnp.where(m == NEG_INF, 0.0, m)
    p_t = jnp.exp2(s - m_safe).T.astype(BF16)
    v_aug = jnp.concatenate([v_cat, jnp.ones((t, LANES), BF16)], axis=1)
    out = jnp.dot(p_t, v_aug, preferred_element_type=F32)
    blocks = [out[h * rows_per_head:(h + 1) * rows_per_head, h * LANES:(h + 1) * LANES] for h in range(heads)]
    if heads * rows_per_head < LANES:
        blocks.append(jnp.zeros((LANES - heads * rows_per_head, LANES), F32))
    return m, jnp.concatenate(blocks, axis=0), out[:, heads * LANES:]


def _paged_partial_body(pt_ref, *refs, pages, heads, rows_per_head, scale, gated):
    del pt_ref
    qbd_ref, refs = refs[0], refs[1:]
    if gated:
        ids_ref, refs = refs[0], refs[1:]
    k_refs, v_refs = refs[:pages], refs[pages:2 * pages]
    m_ref, o_ref = refs[2 * pages:]
    step = pl.program_id(1)
    s = jnp.dot(_gather_heads(k_refs, heads), qbd_ref[0], preferred_element_type=F32) * (scale * LOG2E)
    if gated:
        ids = ids_ref[0]
        blocks_per_step = pages * PAGE_SIZE // MOBA_BLOCK
        parts = []
        for i in range(blocks_per_step):
            blk = step * blocks_per_step + i
            chosen = ids[0:1] == blk
            for k in range(1, MOBA_TOPK):
                chosen = chosen | (ids[k:k + 1] == blk)
            parts.append(jnp.where(chosen, s[i * MOBA_BLOCK:(i + 1) * MOBA_BLOCK], NEG_INF))
        s = jnp.concatenate(parts, axis=0)
    m, o, l = _partial_attention(s, _gather_heads(v_refs, heads), heads, rows_per_head)
    m_ref[0] = jnp.broadcast_to(m, (8, LANES))
    o_ref[0, 0, :, :LANES] = o
    o_ref[0, 0, :, LANES:] = l


def _paged_partials(qbd_bf16, ids, cache_k, cache_v, page_table, heads, rows_per_head, scale, pages_per_step, name):
    b = qbd_bf16.shape[0]
    rows = cache_k.shape[1]
    n_pages = page_table.shape[1]
    assert n_pages % pages_per_step == 0 and (pages_per_step * PAGE_SIZE) % MOBA_BLOCK == 0
    steps = n_pages // pages_per_step

    def page_spec(p):
        return pl.BlockSpec((1, rows, LANES), lambda bi, i, pt: (pt[bi, i * pages_per_step + p], 0, 0))

    per_seq = [pl.BlockSpec((1, heads * LANES, LANES), lambda bi, i, pt: (bi, 0, 0))]
    operands = [qbd_bf16]
    if ids is not None:
        per_seq.append(pl.BlockSpec((1, 8, LANES), lambda bi, i, pt: (bi, 0, 0)))
        operands.append(ids)
    return pl.pallas_call(
        functools.partial(_paged_partial_body, pages=pages_per_step, heads=heads, rows_per_head=rows_per_head,
                          scale=scale, gated=ids is not None),
        grid_spec=pltpu.PrefetchScalarGridSpec(
            num_scalar_prefetch=1,
            grid=(b, steps),
            in_specs=per_seq + [page_spec(p) for p in range(pages_per_step)] * 2,
            out_specs=[pl.BlockSpec((1, 8, LANES), lambda bi, i, pt: (bi, i, 0)),
                       pl.BlockSpec((1, 1, LANES, 2 * LANES), lambda bi, i, pt: (bi, i, 0, 0))],
        ),
        out_shape=[jax.ShapeDtypeStruct((b, steps * 8, LANES), F32),
                   jax.ShapeDtypeStruct((b, steps, LANES, 2 * LANES), F32)],
        compiler_params=_params("parallel", "parallel"),
        name=name,
    )(page_table, *operands, *([cache_k] * pages_per_step), *([cache_v] * pages_per_step))


NEW_TOKEN_ROWS = LANES


def _paged_combine_body(qbd_ref, kn_ref, vn_ref, m_ref, o_ref, *refs, steps, heads, rows_per_head, scale, diff):
    if diff:
        lq1_ref, lk1_ref, lq2_ref, lk2_ref, gs_ref, out_ref = refs
    else:
        (out_ref,) = refs
    nq = kn_ref.shape[1]
    pad = jnp.zeros((NEW_TOKEN_ROWS - nq, heads * LANES), F32)

    kn = jnp.concatenate([kn_ref[0], pad], axis=0).astype(BF16)
    vn = jnp.concatenate([vn_ref[0], pad], axis=0).astype(BF16)
    s = jnp.dot(kn, qbd_ref[0], preferred_element_type=F32) * (scale * LOG2E)
    tok = lax.broadcasted_iota(jnp.int32, s.shape, 0)
    col = lax.broadcasted_iota(jnp.int32, s.shape, 1)
    s = jnp.where(tok <= lax.rem(col, nq), s, NEG_INF)
    m_new, o_new, l_new = _partial_attention(s, vn, heads, rows_per_head)

    m_parts = m_ref[0, pl.ds(0, steps, stride=8), :]
    m_all = jnp.maximum(jnp.max(m_parts, axis=0, keepdims=True), m_new)
    weights = jnp.concatenate(
        [jnp.exp2(m_parts - m_all), jnp.broadcast_to(jnp.exp2(m_new - m_all), (8, LANES)),
         jnp.zeros((LANES - steps - 8, LANES), F32)], axis=0)
    w_t = weights.T
    o = w_t[:, steps:steps + 1] * o_new
    l = w_t[:, steps:steps + 1] * l_new[:, 0:1]
    for i in range(steps):
        part = o_ref[0, i]
        o = o + w_t[:, i:i + 1] * part[:, :LANES]
        l = l + w_t[:, i:i + 1] * part[:, LANES:LANES + 1]
    a = o / l
    if diff:
        lam = _lambda_of(lq1_ref, lk1_ref, lq2_ref, lk2_ref)
    for h in range(heads):
        blk = a[h * rows_per_head:(h + 1) * rows_per_head]
        if diff:
            out_ref[0, :, h * LANES:(h + 1) * LANES] = _subln(blk[:nq] - lam * blk[nq:], gs_ref[...])
        else:
            out_ref[0, :, h * LANES:(h + 1) * LANES] = blk


def _paged_combine(qbd_bf16, k_new, v_new, m_parts, o_parts, heads, rows_per_head, scale, diff_params, name):
    b, nq, w = k_new.shape
    steps = o_parts.shape[1]
    assert steps + 8 <= LANES and nq <= NEW_TOKEN_ROWS
    diff = diff_params is not None
    in_specs = [pl.BlockSpec((1, heads * LANES, LANES), lambda bi: (bi, 0, 0)),
                pl.BlockSpec((1, nq, w), lambda bi: (bi, 0, 0)),
                pl.BlockSpec((1, nq, w), lambda bi: (bi, 0, 0)),
                pl.BlockSpec((1, steps * 8, LANES), lambda bi: (bi, 0, 0)),
                pl.BlockSpec((1, steps, LANES, 2 * LANES), lambda bi: (bi, 0, 0, 0))]
    operands = [qbd_bf16, k_new, v_new, m_parts, o_parts]
    if diff:
        lams, g_subln = diff_params
        in_specs += [pl.BlockSpec((1, LANES // 2), lambda bi: (0, 0))] * 4 + [pl.BlockSpec((1, LANES), lambda bi: (0, 0))]
        operands += [*lams, g_subln.reshape(1, LANES)]
    return pl.pallas_call(
        functools.partial(_paged_combine_body, steps=steps, heads=heads, rows_per_head=rows_per_head, scale=scale,
                          diff=diff),
        grid=(b,),
        in_specs=in_specs,
        out_specs=pl.BlockSpec((1, nq, w), lambda bi: (bi, 0, 0)),
        out_shape=jax.ShapeDtypeStruct((b, nq, w), F32),
        compiler_params=_params("parallel"),
        name=name,
    )(*operands)


def _row_tile(n, want):
    tm = min(n, want)
    while n % tm:
        tm //= 2
    return tm


def kernel(x_prompt, x_sample, cache_k_a, cache_v_a, cache_k_c, cache_v_c, state_pool, page_table, g_norm0, w_in0, w_pool, pool_scale, w_out0, g_norm1, w_in1, lambda_q1, lambda_k1, lambda_q2, lambda_k2, g_subln, w_out1, g_final):
    bp, t, d = x_prompt.shape
    bs, nq, _ = x_sample.shape
    assert bp == 1 and t % MOBA_BLOCK == 0
    n_pool, page, heads, dh = cache_k_a.shape
    assert page == PAGE_SIZE and dh == LANES
    w_a = heads * dh
    w_b = w_pool.shape[0] * w_pool.shape[1]
    w_c = w_out1.shape[0]
    n_pages = page_table.shape[1]
    past = n_pages * PAGE_SIZE
    assert past % MOBA_BLOCK == 0 and past >= POOL_HALO

    w_in0_b, w_out0_b = w_in0.astype(BF16), w_out0.astype(BF16)
    w_in1_b, w_out1_b = w_in1.astype(BF16), w_out1.astype(BF16)
    w_pool_b = w_pool.astype(BF16)
    lams = tuple(v.reshape(1, -1) for v in (lambda_q1, lambda_k1, lambda_q2, lambda_k2))
    caches = [c.reshape(n_pool, page * heads, dh) for c in (cache_k_a, cache_v_a, cache_k_c, cache_v_c)]
    pages_per_step = 8 if n_pages % 8 == 0 else PAGES_PER_BLOCK

    pos_p = jnp.arange(t)
    pos_s = jnp.tile(past + jnp.arange(nq), bs)
    xp = x_prompt.reshape(t, d)
    xs = x_sample.reshape(bs * nq, d)
    tm_p = _row_tile(t, 512)
    tm_s = _row_tile(bs * nq, 256)
    rows, hm, both = ("rows",), ("heads",), ("rows", "heads")

    tab_p, half0 = _rope_tables(pos_p, LANES)
    tab_s, _ = _rope_tables(pos_s, LANES)
    plan0_p = ((w_a, True, rows, False, False), (w_a, True, hm, True, False), (w_a, False, hm, False, True),
               (w_b, False, rows, False, False), (w_a + w_b, False, rows, False, False))
    q0, k0, k0_b, v0, v0_t, u0, gate0 = _proj(xp, g_norm0, w_in0_b, tab_p, half0, plan0_p, tm_p)
    plan0_s = ((w_a, True, rows, False, False), (w_a, True, both, False, False), (w_a, False, both, False, False),
               (w_b, False, rows, False, False), (w_a + w_b, False, rows, False, False))
    q0s, k0s_r, k0s, v0s_r, v0s, u0s, gate0s = _proj(xs, g_norm0, w_in0_b, tab_s, half0, plan0_s, tm_s)

    a_p = _moba_prompt(q0, _block_means(k0, heads), k0_b, v0_t)
    u0_3 = u0.reshape(1, t, w_b)
    pool_tm = _row_tile(t, 512)
    p_p = _pool_mix(u0_3, u0_3, lambda i: jnp.maximum(i * (pool_tm // POOL_HALO) - 1, 0), w_pool_b, pool_scale,
                    pool_tm, 0, True)
    y0_p = _merge_out([a_p, p_p.reshape(t, w_b)], gate0, xp, w_out0_b, g_final, False, tm_p)

    n_blocks = n_pages // PAGES_PER_BLOCK
    means_s = _page_block_means(caches[0], page_table, heads, 8 if n_blocks % 8 == 0 else n_blocks)
    k0s_3, v0s_3 = k0s_r.reshape(bs, nq, w_a), v0s_r.reshape(bs, nq, w_a)
    qbd0 = _block_diag_queries(q0s.reshape(bs, nq, heads, dh).transpose(0, 2, 1, 3), heads, nq)
    qbd0_b = qbd0.astype(BF16)
    ids = _sample_topk(qbd0, means_s, heads)
    m0, o0 = _paged_partials(qbd0_b, ids, caches[0], caches[1], page_table, heads, nq, dh ** -0.5, pages_per_step,
                             "moba_sample_partials")
    a_s = _paged_combine(qbd0_b, k0s_3, v0s_3, m0, o0, heads, nq, dh ** -0.5, None, "moba_sample_combine")
    u0s_3 = u0s.reshape(bs, nq, w_b)
    state_ext = jnp.concatenate([jnp.zeros((bs, POOL_HALO - POOL_BUF, w_b), F32), state_pool], axis=1)
    p_s = _pool_mix(state_ext, u0s_3, lambda i: 0, w_pool_b, pool_scale, nq, past, False)
    y0_s = _merge_out([a_s.reshape(bs * nq, w_a), p_s.reshape(bs * nq, w_b)], gate0s, xs, w_out0_b, g_final, False,
                      tm_s)
    pool_p = u0_3[:, t - POOL_BUF:]
    pool_s = jnp.concatenate([state_pool, u0s_3], axis=1)[:, nq:]

    tab_p1, half1 = _rope_tables(pos_p, LANES // 2)
    tab_s1, _ = _rope_tables(pos_s, LANES // 2)
    plan1_p = ((w_c, True, rows, False, False), (w_c, True, hm, True, False), (w_c, False, hm, False, True),
               (w_c, False, rows, False, False))
    q1, k1, k1_b, v1, v1_t, gate1 = _proj(y0_p, g_norm1, w_in1_b, tab_p1, half1, plan1_p, tm_p)
    plan1_s = ((w_c, True, rows, False, False), (w_c, True, both, False, False), (w_c, False, both, False, False),
               (w_c, False, rows, False, False))
    q1s, k1s_r, k1s, v1s_r, v1s, gate1s = _proj(y0_s, g_norm1, w_in1_b, tab_s1, half1, plan1_s, tm_s)

    o_p = _diff_prompt(q1, k1_b, v1_t, lams, g_subln, _row_tile(t, 256))
    y_p = _merge_out([o_p], gate1, y0_p, w_out1_b, g_final, True, tm_p)

    k1s_3, v1s_3 = k1s_r.reshape(bs, nq, w_c), v1s_r.reshape(bs, nq, w_c)
    q1s_4 = q1s.reshape(bs, nq, heads, dh).transpose(0, 2, 1, 3)
    first = jnp.arange(dh) < dh // 2
    q1s_split = jnp.concatenate([jnp.where(first, q1s_4, 0.0), jnp.where(first, 0.0, q1s_4)], axis=2)
    qbd1_b = _block_diag_queries(q1s_split, heads, 2 * nq).astype(BF16)
    m1, o1 = _paged_partials(qbd1_b, None, caches[2], caches[3], page_table, heads, 2 * nq, (dh // 2) ** -0.5,
                             pages_per_step, "diff_sample_partials")
    o_s = _paged_combine(qbd1_b, k1s_3, v1s_3, m1, o1, heads, 2 * nq, (dh // 2) ** -0.5, (lams, g_subln),
                         "diff_sample_combine")
    y_s = _merge_out([o_s.reshape(bs * nq, w_c)], gate1s, y0_s, w_out1_b, g_final, True, tm_s)

    hd = (heads, dh)
    return (y_p.reshape(1, t, d), y_s.reshape(bs, nq, d),
            k0.reshape(1, t, *hd), v0.reshape(1, t, *hd), pool_p,
            k1.reshape(1, t, *hd), v1.reshape(1, t, *hd),
            k0s.reshape(bs, nq, *hd), v0s.reshape(bs, nq, *hd), pool_s,
            k1s.reshape(bs, nq, *hd), v1s.reshape(bs, nq, *hd))
```

```python
import functools
import math

import jax
import jax.numpy as jnp
from jax import lax
from jax.experimental import pallas as pl
from jax.experimental.pallas import tpu as pltpu

F32 = jnp.float32
BF16 = jnp.bfloat16

LANES = 128
PAGE_SIZE = 128
MOBA_BLOCK = 256
MOBA_TOPK = 3
PAGES_PER_BLOCK = MOBA_BLOCK // PAGE_SIZE
POOL_WINDOWS = (2, 4, 8, 16)
POOL_BUF = max(POOL_WINDOWS) - 1
POOL_HALO = 16
ROPE_THETA = 500000.0
ROPE_FRACTION = 4
RMS_EPS = 1e-6
SUBLN_EPS = 1e-5
LAMBDA_INIT = 0.8 - 0.6 * math.exp(-0.3 * 1)
VMEM_LIMIT = 56 * 1024 * 1024
NEG_INF = float("-inf")


def _params(*sem):
    return pltpu.CompilerParams(dimension_semantics=sem, vmem_limit_bytes=VMEM_LIMIT)


def _rope_tables(pos, period):
    rot = period // ROPE_FRACTION
    half = rot // 2
    inv = ROPE_THETA ** (-(jnp.arange(half, dtype=F32) * 2.0 / rot))
    ang = pos.astype(F32)[:, None] * inv[None, :]
    cos, sin = jnp.cos(ang), jnp.sin(ang)
    n = pos.shape[0]
    zero_h = jnp.zeros((n, half), F32)
    rest0 = jnp.zeros((n, period - rot), F32)
    c = jnp.concatenate([cos, cos, jnp.ones((n, period - rot), F32)], axis=1)
    s1 = jnp.concatenate([-sin, zero_h, rest0], axis=1)
    s2 = jnp.concatenate([zero_h, sin, rest0], axis=1)
    reps = LANES // period
    return tuple(jnp.tile(t, (1, reps)) for t in (c, s1, s2)), half


def _proj_body(x_ref, g_ref, w_ref, c_ref, s1_ref, s2_ref, *out_refs, plan, half):
    x = x_ref[...]
    ms = jnp.mean(x * x, axis=-1, keepdims=True)
    xn = (x * lax.rsqrt(ms + RMS_EPS) * g_ref[...]).astype(BF16)
    tm = x.shape[0]
    outs = iter(out_refs)
    off = 0
    for width, rope, want_f32, want_bf16, want_bf16_t in plan:
        y = jnp.dot(xn, w_ref[:, off:off + width], preferred_element_type=F32)
        off += width
        heads = width // LANES
        if rope:
            c, s1, s2 = c_ref[...], s1_ref[...], s2_ref[...]
            blocks = []
            for h in range(heads):
                blk = y[:, h * LANES:(h + 1) * LANES]
                blocks.append(blk * c + pltpu.roll(blk, LANES - half, 1) * s1 + pltpu.roll(blk, half, 1) * s2)
            y = jnp.concatenate(blocks, axis=1)
        if "rows" in want_f32:
            next(outs)[...] = y
        if "heads" in want_f32:
            o_ref = next(outs)
            for h in range(heads):
                o_ref[pl.ds(h, tm, stride=heads), :] = y[:, h * LANES:(h + 1) * LANES]
        if want_bf16:
            next(outs)[...] = y.astype(BF16)
        if want_bf16_t:
            next(outs)[...] = y.T.astype(BF16)


def _proj(x, g, w_bf16, tables, half, plan, tm):
    n, d = x.shape
    p = w_bf16.shape[1]
    assert n % tm == 0 and sum(seg[0] for seg in plan) == p
    out_shape, out_specs = [], []
    for width, _, want_f32, want_bf16, want_bf16_t in plan:
        if "rows" in want_f32:
            out_shape.append(jax.ShapeDtypeStruct((n, width), F32))
            out_specs.append(pl.BlockSpec((tm, width), lambda i: (i, 0)))
        if "heads" in want_f32:
            heads = width // LANES
            out_shape.append(jax.ShapeDtypeStruct((n * heads, LANES), F32))
            out_specs.append(pl.BlockSpec((tm * heads, LANES), lambda i: (i, 0)))
        if want_bf16:
            out_shape.append(jax.ShapeDtypeStruct((n, width), BF16))
            out_specs.append(pl.BlockSpec((tm, width), lambda i: (i, 0)))
        if want_bf16_t:
            out_shape.append(jax.ShapeDtypeStruct((width, n), BF16))
            out_specs.append(pl.BlockSpec((width, tm), lambda i: (0, i)))
    row = pl.BlockSpec((tm, LANES), lambda i: (i, 0))
    return pl.pallas_call(
        functools.partial(_proj_body, plan=plan, half=half),
        grid=(n // tm,),
        in_specs=[pl.BlockSpec((tm, d), lambda i: (i, 0)),
                  pl.BlockSpec((1, d), lambda i: (0, 0)),
                  pl.BlockSpec((d, p), lambda i: (0, 0)),
                  row, row, row],
        out_specs=out_specs,
        out_shape=out_shape,
        compiler_params=_params("parallel"),
        name="proj_in",
    )(x, g.reshape(1, d), w_bf16, *tables)


def _block_mean_body(k_ref, o_ref, *, heads):
    k = k_ref[...].reshape(MOBA_BLOCK, heads, LANES)
    o_ref[...] = jnp.sum(k, axis=0) * (1.0 / MOBA_BLOCK)


def _block_means(k_heads, heads):
    t = k_heads.shape[0] // heads
    nb = t // MOBA_BLOCK
    return pl.pallas_call(
        functools.partial(_block_mean_body, heads=heads),
        grid=(nb,),
        in_specs=[pl.BlockSpec((MOBA_BLOCK * heads, LANES), lambda i: (i, 0))],
        out_specs=pl.BlockSpec((heads, LANES), lambda i: (i, 0)),
        out_shape=jax.ShapeDtypeStruct((nb * heads, LANES), F32),
        compiler_params=_params("parallel"),
        name="moba_block_means",
    )(k_heads)


def _top_blocks(gate_t, n_valid, nb):
    row = lax.broadcasted_iota(jnp.int32, gate_t.shape, 0)
    g = jnp.where(row < n_valid, gate_t, NEG_INF)
    sel = jnp.zeros(gate_t.shape, F32)
    for _ in range(MOBA_TOPK):
        m = jnp.max(g, axis=0, keepdims=True)
        cand = jnp.where((g == m) & (g > NEG_INF), row, nb)
        idx = jnp.min(cand, axis=0, keepdims=True)
        pick = row == idx
        sel = jnp.where(pick, 1.0, sel)
        g = jnp.where(pick, NEG_INF, g)
    return sel


SUM_ROWS = 16


def _pv(v_t, p):
    v_aug = jnp.concatenate([v_t, jnp.ones((SUM_ROWS, v_t.shape[1]), BF16)], axis=0)
    return jnp.dot(v_aug, p.astype(BF16), preferred_element_type=F32)


def _online_update(s, v_t, m_i, acc):
    m_new = jnp.maximum(m_i, jnp.max(s, axis=0, keepdims=True))
    return m_new, acc * jnp.exp2(m_i - m_new) + _pv(v_t, jnp.exp2(s - m_new))


def _online_update_guarded(s, v_t, m_i, acc):
    m_new = jnp.maximum(m_i, jnp.max(s, axis=0, keepdims=True))
    m_safe = jnp.where(m_new == NEG_INF, 0.0, m_new)
    return m_new, acc * jnp.exp2(m_i - m_safe) + _pv(v_t, jnp.exp2(s - m_safe))


def _fresh_state(cols):
    return jnp.full((1, cols), NEG_INF, F32), jnp.zeros((LANES + SUM_ROWS, cols), F32)


def _normalised(acc):
    return acc[:LANES] / acc[LANES:LANES + 1]


_NT = (((1,), (1,)), ((), ()))
LOG2E = math.log2(math.e)
FLASH_UNROLLS = (16, 8, 4, 2, 1)


def _flash_tiles(scores, values_t, mask_of, n_tiles, state, s_ref, update):
    done = 0
    for width in FLASH_UNROLLS:
        trips = (n_tiles - done) // width
        base = done

        def step(i, state, width=width, base=base):
            n = base + width * i
            for u in range(width):
                s_ref[(u + 1) % 2] = scores(n + u + 1)
                s = s_ref[u % 2]
                mask = mask_of(n + u)
                if mask is not None:
                    s = jnp.where(mask, s, NEG_INF)
                state = update(s, values_t(n + u), *state)
            return state

        state = lax.fori_loop(0, trips, step, state)
        done = done + trips * width
    return state


def _moba_prompt_body(q_ref, means_ref, k_ref, vt_ref, o_ref, sel_ref, s_ref, *, nb, heads, scale):
    pair = pl.program_id(1)
    blk = MOBA_BLOCK
    first = 2 * pair
    q = q_ref[...]
    means = means_ref[pl.ds(pl.program_id(0), nb, stride=heads), :]
    gate_t = lax.dot_general(means, q, _NT, precision=lax.Precision.HIGHEST, preferred_element_type=F32)
    col = lax.broadcasted_iota(jnp.int32, (1, 2 * blk), 1)
    sel_ref[...] = _top_blocks(gate_t, first + (col >= blk).astype(jnp.int32), nb)
    qs = (q * (scale * LOG2E)).astype(BF16)

    def scores(j):
        kj = k_ref[pl.ds(pl.multiple_of(j * blk, blk), blk), :]
        return lax.dot_general(kj, qs, _NT, preferred_element_type=F32)

    def values_t(j):
        return vt_ref[:, pl.ds(pl.multiple_of(j * blk, blk), blk)]

    key = lax.broadcasted_iota(jnp.int32, (blk, blk), 0)
    qry = lax.broadcasted_iota(jnp.int32, (blk, blk), 1)
    causal = key <= qry
    chosen = jnp.broadcast_to(sel_ref[pl.ds(first, 1), blk:] > 0.0, (blk, blk))
    nothing = jnp.zeros((blk, blk), jnp.bool_)
    s_own0 = jnp.where(jnp.concatenate([causal, chosen], axis=1), scores(first), NEG_INF)
    s_own1 = jnp.where(jnp.concatenate([nothing, causal], axis=1), scores(first + 1), NEG_INF)
    s_ref[0] = scores(0)
    state = _online_update_guarded(s_own0, values_t(first), *_fresh_state(2 * blk))
    state = _online_update_guarded(s_own1, values_t(first + 1), *state)
    _, acc = _flash_tiles(scores, values_t, lambda j: sel_ref[pl.ds(j, 1), :] > 0.0, first, state, s_ref,
                          _online_update)
    o_ref[...] = _normalised(acc).T


def _moba_prompt(q, means, k_bf16, vt_bf16):
    t, w = q.shape
    nb = t // MOBA_BLOCK
    heads = w // LANES
    assert nb % 2 == 0
    return pl.pallas_call(
        functools.partial(_moba_prompt_body, nb=nb, heads=heads, scale=LANES ** -0.5),
        grid=(heads, nb // 2),
        in_specs=[pl.BlockSpec((2 * MOBA_BLOCK, LANES), lambda h, i: (i, h)),
                  pl.BlockSpec((nb * heads, LANES), lambda h, i: (0, 0)),
                  pl.BlockSpec((t, LANES), lambda h, i: (0, h), pipeline_mode=pl.Buffered(1)),
                  pl.BlockSpec((LANES, t), lambda h, i: (h, 0), pipeline_mode=pl.Buffered(1))],
        out_specs=pl.BlockSpec((2 * MOBA_BLOCK, LANES), lambda h, i: (i, h)),
        out_shape=jax.ShapeDtypeStruct((t, w), F32),
        scratch_shapes=[pltpu.VMEM((nb, 2 * MOBA_BLOCK), F32), pltpu.VMEM((2, MOBA_BLOCK, 2 * MOBA_BLOCK), F32)],
        compiler_params=_params("parallel", "arbitrary"),
        name="moba_prompt",
    )(q, means, k_bf16, vt_bf16)


def _pool_body(prev_ref, cur_ref, w_ref, scale_ref, o_ref, ext_ref, *, tm, pos0, zero_first_halo):
    i = pl.program_id(1)
    prev = prev_ref[0]
    if zero_first_halo:
        prev = jnp.where(i == 0, 0.0, prev)
    ext_ref[0:POOL_HALO, :] = prev
    ext_ref[POOL_HALO:POOL_HALO + tm, :] = cur_ref[0]
    pos = pos0 + i * tm + lax.broadcasted_iota(jnp.int32, (tm, 1), 0)
    for g, win in enumerate(POOL_WINDOWS):
        cols = slice(g * LANES, (g + 1) * LANES)
        tok = ext_ref[POOL_HALO:POOL_HALO + tm, cols]
        tot = tok
        for back in range(1, win):
            tot = tot + ext_ref[POOL_HALO - back:POOL_HALO - back + tm, cols]
        count = jnp.minimum(pos + 1, win).astype(F32)
        dlt = (tot / count - tok).astype(BF16)
        y = jnp.dot(dlt, w_ref[g], preferred_element_type=F32)
        o_ref[0, :, cols] = y * scale_ref[:, cols]


def _pool_mix(prev, cur, prev_block_of, w_pool_bf16, pool_scale, tm, pos0, zero_first_halo):
    b, t, w = cur.shape
    assert t % tm == 0
    return pl.pallas_call(
        functools.partial(_pool_body, tm=tm, pos0=pos0, zero_first_halo=zero_first_halo),
        grid=(b, t // tm),
        in_specs=[pl.BlockSpec((1, POOL_HALO, w), lambda bi, i: (bi, prev_block_of(i), 0)),
                  pl.BlockSpec((1, tm, w), lambda bi, i: (bi, i, 0)),
                  pl.BlockSpec(w_pool_bf16.shape, lambda bi, i: (0, 0, 0)),
                  pl.BlockSpec((1, w), lambda bi, i: (0, 0))],
        out_specs=pl.BlockSpec((1, tm, w), lambda bi, i: (bi, i, 0)),
        out_shape=jax.ShapeDtypeStruct((b, t, w), F32),
        scratch_shapes=[pltpu.VMEM((POOL_HALO + tm, w), F32)],
        compiler_params=_params("parallel", "parallel"),
        name="pool_mix",
    )(prev, cur, w_pool_bf16, pool_scale.reshape(1, w))


def _silu(g):
    return g / (1.0 + jnp.exp(-g))


def _merge_body(*refs, n_parts, final_norm):
    part_refs = refs[:n_parts]
    gate_ref, x_ref, w_ref, gf_ref, o_ref = refs[n_parts:]
    y = x_ref[...]
    off = 0
    for p_ref in part_refs:
        width = p_ref.shape[1]
        h = (p_ref[...] * _silu(gate_ref[:, off:off + width])).astype(BF16)
        y = y + jnp.dot(h, w_ref[off:off + width, :], preferred_element_type=F32)
        off += width
    if final_norm:
        ms = jnp.mean(y * y, axis=-1, keepdims=True)
        y = y * lax.rsqrt(ms + RMS_EPS) * gf_ref[...]
    o_ref[...] = y


def _merge_out(parts, gate, x, w_out_bf16, g_final, final_norm, tm):
    n, d = x.shape
    assert n % tm == 0
    wsum = w_out_bf16.shape[0]
    in_specs = [pl.BlockSpec((tm, p.shape[1]), lambda i: (i, 0)) for p in parts]
    in_specs += [pl.BlockSpec((tm, wsum), lambda i: (i, 0)),
                 pl.BlockSpec((tm, d), lambda i: (i, 0)),
                 pl.BlockSpec((wsum, d), lambda i: (0, 0)),
                 pl.BlockSpec((1, d), lambda i: (0, 0))]
    return pl.pallas_call(
        functools.partial(_merge_body, n_parts=len(parts), final_norm=final_norm),
        grid=(n // tm,),
        in_specs=in_specs,
        out_specs=pl.BlockSpec((tm, d), lambda i: (i, 0)),
        out_shape=jax.ShapeDtypeStruct((n, d), F32),
        compiler_params=_params("parallel"),
        name="merge_out",
    )(*parts, gate, x, w_out_bf16, g_final.reshape(1, d))


def _lambda_of(lq1_ref, lk1_ref, lq2_ref, lk2_ref):
    a = jnp.sum(lq1_ref[...] * lk1_ref[...], axis=-1, keepdims=True)
    b = jnp.sum(lq2_ref[...] * lk2_ref[...], axis=-1, keepdims=True)
    return jnp.exp(a) - jnp.exp(b) + LAMBDA_INIT


def _split_components(q, scale):
    lane = lax.broadcasted_iota(jnp.int32, q.shape, 1)
    qs = q * scale
    half = LANES // 2
    return jnp.concatenate([jnp.where(lane < half, qs, 0.0), jnp.where(lane >= half, qs, 0.0)], axis=0).astype(BF16)


def _subln(o, g_row):
    ms = jnp.mean(o * o, axis=-1, keepdims=True)
    return o * lax.rsqrt(ms + SUBLN_EPS) * g_row * (1.0 - LAMBDA_INIT)


def _diff_prompt_body(q_ref, k_ref, vt_ref, lq1_ref, lk1_ref, lq2_ref, lk2_ref, gs_ref, o_ref, s_ref, *, tq, scale):
    qi = pl.program_id(1)
    qz = _split_components(q_ref[...], scale * LOG2E)

    def scores(j):
        kj = k_ref[pl.ds(pl.multiple_of(j * tq, tq), tq), :]
        return lax.dot_general(kj, qz, _NT, preferred_element_type=F32)

    def values_t(j):
        return vt_ref[:, pl.ds(pl.multiple_of(j * tq, tq), tq)]

    s_ref[0] = scores(0)
    state = _flash_tiles(scores, values_t, lambda j: None, qi, _fresh_state(2 * tq), s_ref, _online_update)
    key = lax.broadcasted_iota(jnp.int32, (tq, tq), 0)
    qry = lax.broadcasted_iota(jnp.int32, (tq, tq), 1)
    causal = key <= qry
    s_own = jnp.where(jnp.concatenate([causal, causal], axis=1), s_ref[qi % 2], NEG_INF)
    _, acc = _online_update(s_own, values_t(qi), *state)
    a = _normalised(acc)
    lam = _lambda_of(lq1_ref, lk1_ref, lq2_ref, lk2_ref)
    o = (a[:, :tq] - lam * a[:, tq:]).T
    o_ref[...] = _subln(o, gs_ref[...])


def _diff_prompt(q, k_bf16, vt_bf16, lams, g_subln, tq):
    t, w = q.shape
    heads = w // LANES
    lam_spec = pl.BlockSpec((1, LANES // 2), lambda h, i: (0, 0))
    return pl.pallas_call(
        functools.partial(_diff_prompt_body, tq=tq, scale=(LANES // 2) ** -0.5),
        grid=(heads, t // tq),
        in_specs=[pl.BlockSpec((tq, LANES), lambda h, i: (i, h)),
                  pl.BlockSpec((t, LANES), lambda h, i: (0, h), pipeline_mode=pl.Buffered(1)),
                  pl.BlockSpec((LANES, t), lambda h, i: (h, 0), pipeline_mode=pl.Buffered(1)),
                  lam_spec, lam_spec, lam_spec, lam_spec,
                  pl.BlockSpec((1, LANES), lambda h, i: (0, 0))],
        out_specs=pl.BlockSpec((tq, LANES), lambda h, i: (i, h)),
        out_shape=jax.ShapeDtypeStruct((t, w), F32),
        scratch_shapes=[pltpu.VMEM((2, tq, 2 * tq), F32)],
        compiler_params=_params("parallel", "arbitrary"),
        name="diff_prompt",
    )(q, k_bf16, vt_bf16, *lams, g_subln.reshape(1, LANES))


def _head_rows(page_ref, h, heads):
    return page_ref[0, pl.ds(h, PAGE_SIZE, stride=heads), :]


def _page_means_body(pt_ref, *refs, blocks, heads):
    del pt_ref
    page_refs, o_ref = refs[:-1], refs[-1]
    for b in range(blocks):
        tot = None
        for p in range(PAGES_PER_BLOCK):
            page = page_refs[b * PAGES_PER_BLOCK + p][0].reshape(PAGE_SIZE, heads, LANES)
            part = jnp.sum(page, axis=0)
            tot = part if tot is None else tot + part
        o_ref[0, b * heads:(b + 1) * heads, :] = tot * (1.0 / MOBA_BLOCK)


def _page_block_means(cache_k, page_table, heads, blocks_per_step):
    _, rows, _ = cache_k.shape
    b, n_pages = page_table.shape
    n_blocks = n_pages // PAGES_PER_BLOCK
    assert n_blocks % blocks_per_step == 0
    pages_per_step = blocks_per_step * PAGES_PER_BLOCK

    def page_spec(p):
        return pl.BlockSpec((1, rows, LANES), lambda bi, i, pt: (pt[bi, i * pages_per_step + p], 0, 0))

    return pl.pallas_call(
        functools.partial(_page_means_body, blocks=blocks_per_step, heads=heads),
        grid_spec=pltpu.PrefetchScalarGridSpec(
            num_scalar_prefetch=1,
            grid=(b, n_blocks // blocks_per_step),
            in_specs=[page_spec(p) for p in range(pages_per_step)],
            out_specs=pl.BlockSpec((1, blocks_per_step * heads, LANES), lambda bi, i, pt: (bi, i, 0)),
        ),
        out_shape=jax.ShapeDtypeStruct((b, n_blocks * heads, LANES), F32),
        compiler_params=_params("parallel", "parallel"),
        name="page_block_means",
    )(page_table, *([cache_k] * pages_per_step))


def _block_diag_queries(q, heads, rows_per_head):
    b = q.shape[0]
    eye = jnp.eye(heads, dtype=q.dtype)
    out = jnp.einsum("bhrd,hg->bhdgr", q, eye).reshape(b, heads * LANES, heads * rows_per_head)
    return jnp.pad(out, ((0, 0), (0, 0), (0, LANES - heads * rows_per_head)))


def _gather_heads(page_refs, heads):
    return jnp.concatenate(
        [jnp.concatenate([_head_rows(r, h, heads) for r in page_refs], axis=0).astype(BF16) for h in range(heads)],
        axis=1)


def _sample_topk_body(qbd_ref, means_ref, o_ref, *, heads, nb):
    means = jnp.concatenate([means_ref[0, pl.ds(h, nb, stride=heads), :] for h in range(heads)], axis=1)
    g = jnp.dot(means, qbd_ref[0], precision=lax.Precision.HIGHEST, preferred_element_type=F32)
    row = lax.broadcasted_iota(jnp.int32, g.shape, 0)
    slot = lax.broadcasted_iota(jnp.int32, (8, LANES), 0)
    ids = jnp.full((8, LANES), -1, jnp.int32)
    for k in range(MOBA_TOPK):
        m = jnp.max(g, axis=0, keepdims=True)
        idx = jnp.min(jnp.where(g == m, row, nb - 1), axis=0, keepdims=True)
        ids = jnp.where(slot == k, idx, ids)
        g = jnp.where(row == idx, NEG_INF, g)
    o_ref[0] = ids


def _sample_topk(qbd, means, heads):
    b = qbd.shape[0]
    nb = means.shape[1] // heads
    assert nb >= MOBA_TOPK
    return pl.pallas_call(
        functools.partial(_sample_topk_body, heads=heads, nb=nb),
        grid=(b,),
        in_specs=[pl.BlockSpec((1, heads * LANES, LANES), lambda bi: (bi, 0, 0)),
                  pl.BlockSpec((1, nb * heads, LANES), lambda bi: (bi, 0, 0))],
        out_specs=pl.BlockSpec((1, 8, LANES), lambda bi: (bi, 0, 0)),
        out_shape=jax.ShapeDtypeStruct((b, 8, LANES), jnp.int32),
        compiler_params=_params("parallel"),
        name="sample_topk",
    )(qbd, means)


def _partial_attention(s, v_cat, heads, rows_per_head):
    t = s.shape[0]
    m = jnp.max(s, axis=0, keepdims=True)
    m_safe = jnp.where(m == NEG_INF, 0.0, m)
    p_t = jnp.exp2(s - m_safe).T.astype(BF16)
    v_aug = jnp.concatenate([v_cat, jnp.ones((t, LANES), BF16)], axis=1)
    out = jnp.dot(p_t, v_aug, preferred_element_type=F32)
    blocks = [out[h * rows_per_head:(h + 1) * rows_per_head, h * LANES:(h + 1) * LANES] for h in range(heads)]
    if heads * rows_per_head < LANES:
        blocks.append(jnp.zeros((LANES - heads * rows_per_head, LANES), F32))
    return m, jnp.concatenate(blocks, axis=0), out[:, heads * LANES:]


PARTIAL_GROUPS = 1


def _paged_partial_body(pt_ref, *refs, pages, heads, rows_per_head, scale, gated):
    del pt_ref
    qbd_ref, refs = refs[0], refs[1:]
    if gated:
        ids_ref, refs = refs[0], refs[1:]
    k_refs, v_refs = refs[:pages], refs[pages:2 * pages]
    m_ref, o_ref = refs[2 * pages:]
    step = pl.program_id(1)
    group_pages = pages // PARTIAL_GROUPS
    blocks_per_group = group_pages * PAGE_SIZE // MOBA_BLOCK
    for g in range(PARTIAL_GROUPS):
        mine = slice(g * group_pages, (g + 1) * group_pages)
        s = jnp.dot(_gather_heads(k_refs[mine], heads), qbd_ref[0], preferred_element_type=F32) * (scale * LOG2E)
        if gated:
            ids = ids_ref[0]
            parts = []
            for i in range(blocks_per_group):
                blk = (step * PARTIAL_GROUPS + g) * blocks_per_group + i
                chosen = ids[0:1] == blk
                for k in range(1, MOBA_TOPK):
                    chosen = chosen | (ids[k:k + 1] == blk)
                parts.append(jnp.where(chosen, s[i * MOBA_BLOCK:(i + 1) * MOBA_BLOCK], NEG_INF))
            s = jnp.concatenate(parts, axis=0)
        m, o, l = _partial_attention(s, _gather_heads(v_refs[mine], heads), heads, rows_per_head)
        m_ref[0, g * 8:(g + 1) * 8] = jnp.broadcast_to(m, (8, LANES))
        o_ref[0, g, :, :LANES] = o
        o_ref[0, g, :, LANES:] = l


def _paged_partials(qbd_bf16, ids, cache_k, cache_v, page_table, heads, rows_per_head, scale, pages_per_step, name):
    b = qbd_bf16.shape[0]
    rows = cache_k.shape[1]
    n_pages = page_table.shape[1]
    assert n_pages % pages_per_step == 0 and pages_per_step % PARTIAL_GROUPS == 0
    assert (pages_per_step // PARTIAL_GROUPS * PAGE_SIZE) % MOBA_BLOCK == 0
    steps = n_pages // pages_per_step
    groups = PARTIAL_GROUPS

    def page_spec(p):
        return pl.BlockSpec((1, rows, LANES), lambda bi, i, pt: (pt[bi, i * pages_per_step + p], 0, 0))

    per_seq = [pl.BlockSpec((1, heads * LANES, LANES), lambda bi, i, pt: (bi, 0, 0))]
    operands = [qbd_bf16]
    if ids is not None:
        per_seq.append(pl.BlockSpec((1, 8, LANES), lambda bi, i, pt: (bi, 0, 0)))
        operands.append(ids)
    return pl.pallas_call(
        functools.partial(_paged_partial_body, pages=pages_per_step, heads=heads, rows_per_head=rows_per_head,
                          scale=scale, gated=ids is not None),
        grid_spec=pltpu.PrefetchScalarGridSpec(
            num_scalar_prefetch=1,
            grid=(b, steps),
            in_specs=per_seq + [page_spec(p) for p in range(pages_per_step)] * 2,
            out_specs=[pl.BlockSpec((1, groups * 8, LANES), lambda bi, i, pt: (bi, i, 0)),
                       pl.BlockSpec((1, groups, LANES, 2 * LANES), lambda bi, i, pt: (bi, i, 0, 0))],
        ),
        out_shape=[jax.ShapeDtypeStruct((b, steps * groups * 8, LANES), F32),
                   jax.ShapeDtypeStruct((b, steps * groups, LANES, 2 * LANES), F32)],
        compiler_params=_params("parallel", "parallel"),
        name=name,
    )(page_table, *operands, *([cache_k] * pages_per_step), *([cache_v] * pages_per_step))


NEW_TOKEN_ROWS = LANES


def _paged_combine_body(qbd_ref, kn_ref, vn_ref, m_ref, o_ref, *refs, steps, heads, rows_per_head, scale, diff):
    if diff:
        lq1_ref, lk1_ref, lq2_ref, lk2_ref, gs_ref, out_ref = refs
    else:
        (out_ref,) = refs
    nq = kn_ref.shape[1]
    pad = jnp.zeros((NEW_TOKEN_ROWS - nq, heads * LANES), F32)

    kn = jnp.concatenate([kn_ref[0], pad], axis=0).astype(BF16)
    vn = jnp.concatenate([vn_ref[0], pad], axis=0).astype(BF16)
    s = jnp.dot(kn, qbd_ref[0], preferred_element_type=F32) * (scale * LOG2E)
    tok = lax.broadcasted_iota(jnp.int32, s.shape, 0)
    col = lax.broadcasted_iota(jnp.int32, s.shape, 1)
    s = jnp.where(tok <= lax.rem(col, nq), s, NEG_INF)
    m_new, o_new, l_new = _partial_attention(s, vn, heads, rows_per_head)

    m_parts = m_ref[0, pl.ds(0, steps, stride=8), :]
    m_all = jnp.maximum(jnp.max(m_parts, axis=0, keepdims=True), m_new)
    weights = jnp.concatenate(
        [jnp.exp2(m_parts - m_all), jnp.broadcast_to(jnp.exp2(m_new - m_all), (8, LANES)),
         jnp.zeros((LANES - steps - 8, LANES), F32)], axis=0)
    w_t = weights.T
    o = w_t[:, steps:steps + 1] * o_new
    l = w_t[:, steps:steps + 1] * l_new[:, 0:1]
    for i in range(steps):
        part = o_ref[0, i]
        o = o + w_t[:, i:i + 1] * part[:, :LANES]
        l = l + w_t[:, i:i + 1] * part[:, LANES:LANES + 1]
    a = o / l
    if diff:
        lam = _lambda_of(lq1_ref, lk1_ref, lq2_ref, lk2_ref)
    for h in range(heads):
        blk = a[h * rows_per_head:(h + 1) * rows_per_head]
        if diff:
            out_ref[0, :, h * LANES:(h + 1) * LANES] = _subln(blk[:nq] - lam * blk[nq:], gs_ref[...])
        else:
            out_ref[0, :, h * LANES:(h + 1) * LANES] = blk


def _paged_combine(qbd_bf16, k_new, v_new, m_parts, o_parts, heads, rows_per_head, scale, diff_params, name):
    b, nq, w = k_new.shape
    steps = o_parts.shape[1]
    assert steps + 8 <= LANES and nq <= NEW_TOKEN_ROWS
    diff = diff_params is not None
    in_specs = [pl.BlockSpec((1, heads * LANES, LANES), lambda bi: (bi, 0, 0)),
                pl.BlockSpec((1, nq, w), lambda bi: (bi, 0, 0)),
                pl.BlockSpec((1, nq, w), lambda bi: (bi, 0, 0)),
                pl.BlockSpec((1, steps * 8, LANES), lambda bi: (bi, 0, 0)),
                pl.BlockSpec((1, steps, LANES, 2 * LANES), lambda bi: (bi, 0, 0, 0))]
    operands = [qbd_bf16, k_new, v_new, m_parts, o_parts]
    if diff:
        lams, g_subln = diff_params
        in_specs += [pl.BlockSpec((1, LANES // 2), lambda bi: (0, 0))] * 4 + [pl.BlockSpec((1, LANES), lambda bi: (0, 0))]
        operands += [*lams, g_subln.reshape(1, LANES)]
    return pl.pallas_call(
        functools.partial(_paged_combine_body, steps=steps, heads=heads, rows_per_head=rows_per_head, scale=scale,
                          diff=diff),
        grid=(b,),
        in_specs=in_specs,
        out_specs=pl.BlockSpec((1, nq, w), lambda bi: (bi, 0, 0)),
        out_shape=jax.ShapeDtypeStruct((b, nq, w), F32),
        compiler_params=_params("parallel"),
        name=name,
    )(*operands)


def _row_tile(n, want):
    tm = min(n, want)
    while n % tm:
        tm //= 2
    return tm


def kernel(x_prompt, x_sample, cache_k_a, cache_v_a, cache_k_c, cache_v_c, state_pool, page_table, g_norm0, w_in0, w_pool, pool_scale, w_out0, g_norm1, w_in1, lambda_q1, lambda_k1, lambda_q2, lambda_k2, g_subln, w_out1, g_final):
    bp, t, d = x_prompt.shape
    bs, nq, _ = x_sample.shape
    assert bp == 1 and t % MOBA_BLOCK == 0
    n_pool, page, heads, dh = cache_k_a.shape
    assert page == PAGE_SIZE and dh == LANES
    w_a = heads * dh
    w_b = w_pool.shape[0] * w_pool.shape[1]
    w_c = w_out1.shape[0]
    n_pages = page_table.shape[1]
    past = n_pages * PAGE_SIZE
    assert past % MOBA_BLOCK == 0 and past >= POOL_HALO

    w_in0_b, w_out0_b = w_in0.astype(BF16), w_out0.astype(BF16)
    w_in1_b, w_out1_b = w_in1.astype(BF16), w_out1.astype(BF16)
    w_pool_b = w_pool.astype(BF16)
    lams = tuple(v.reshape(1, -1) for v in (lambda_q1, lambda_k1, lambda_q2, lambda_k2))
    caches = [c.reshape(n_pool, page * heads, dh) for c in (cache_k_a, cache_v_a, cache_k_c, cache_v_c)]
    pages_per_step = 16 if n_pages % 16 == 0 else PAGES_PER_BLOCK * PARTIAL_GROUPS

    pos_p = jnp.arange(t)
    pos_s = jnp.tile(past + jnp.arange(nq), bs)
    xp = x_prompt.reshape(t, d)
    xs = x_sample.reshape(bs * nq, d)
    tm_p = _row_tile(t, 512)
    tm_s = _row_tile(bs * nq, 256)
    rows, hm, both = ("rows",), ("heads",), ("rows", "heads")

    tab_p, half0 = _rope_tables(pos_p, LANES)
    tab_s, _ = _rope_tables(pos_s, LANES)
    plan0_p = ((w_a, True, rows, False, False), (w_a, True, hm, True, False), (w_a, False, hm, False, True),
               (w_b, False, rows, False, False), (w_a + w_b, False, rows, False, False))
    q0, k0, k0_b, v0, v0_t, u0, gate0 = _proj(xp, g_norm0, w_in0_b, tab_p, half0, plan0_p, tm_p)
    plan0_s = ((w_a, True, rows, False, False), (w_a, True, both, False, False), (w_a, False, both, False, False),
               (w_b, False, rows, False, False), (w_a + w_b, False, rows, False, False))
    q0s, k0s_r, k0s, v0s_r, v0s, u0s, gate0s = _proj(xs, g_norm0, w_in0_b, tab_s, half0, plan0_s, tm_s)

    a_p = _moba_prompt(q0, _block_means(k0, heads), k0_b, v0_t)
    u0_3 = u0.reshape(1, t, w_b)
    pool_tm = _row_tile(t, 512)
    p_p = _pool_mix(u0_3, u0_3, lambda i: jnp.maximum(i * (pool_tm // POOL_HALO) - 1, 0), w_pool_b, pool_scale,
                    pool_tm, 0, True)
    y0_p = _merge_out([a_p, p_p.reshape(t, w_b)], gate0, xp, w_out0_b, g_final, False, tm_p)

    n_blocks = n_pages // PAGES_PER_BLOCK
    means_s = _page_block_means(caches[0], page_table, heads, 8 if n_blocks % 8 == 0 else n_blocks)
    k0s_3, v0s_3 = k0s_r.reshape(bs, nq, w_a), v0s_r.reshape(bs, nq, w_a)
    qbd0 = _block_diag_queries(q0s.reshape(bs, nq, heads, dh).transpose(0, 2, 1, 3), heads, nq)
    qbd0_b = qbd0.astype(BF16)
    ids = _sample_topk(qbd0, means_s, heads)
    m0, o0 = _paged_partials(qbd0_b, ids, caches[0], caches[1], page_table, heads, nq, dh ** -0.5, pages_per_step,
                             "moba_sample_partials")
    a_s = _paged_combine(qbd0_b, k0s_3, v0s_3, m0, o0, heads, nq, dh ** -0.5, None, "moba_sample_combine")
    u0s_3 = u0s.reshape(bs, nq, w_b)
    state_ext = jnp.concatenate([jnp.zeros((bs, POOL_HALO - POOL_BUF, w_b), F32), state_pool], axis=1)
    p_s = _pool_mix(state_ext, u0s_3, lambda i: 0, w_pool_b, pool_scale, nq, past, False)
    y0_s = _merge_out([a_s.reshape(bs * nq, w_a), p_s.reshape(bs * nq, w_b)], gate0s, xs, w_out0_b, g_final, False,
                      tm_s)
    pool_p = u0_3[:, t - POOL_BUF:]
    pool_s = jnp.concatenate([state_pool, u0s_3], axis=1)[:, nq:]

    tab_p1, half1 = _rope_tables(pos_p, LANES // 2)
    tab_s1, _ = _rope_tables(pos_s, LANES // 2)
    plan1_p = ((w_c, True, rows, False, False), (w_c, True, hm, True, False), (w_c, False, hm, False, True),
               (w_c, False, rows, False, False))
    q1, k1, k1_b, v1, v1_t, gate1 = _proj(y0_p, g_norm1, w_in1_b, tab_p1, half1, plan1_p, tm_p)
    plan1_s = ((w_c, True, rows, False, False), (w_c, True, both, False, False), (w_c, False, both, False, False),
               (w_c, False, rows, False, False))
    q1s, k1s_r, k1s, v1s_r, v1s, gate1s = _proj(y0_s, g_norm1, w_in1_b, tab_s1, half1, plan1_s, tm_s)

    o_p = _diff_prompt(q1, k1_b, v1_t, lams, g_subln, _row_tile(t, 256))
    y_p = _merge_out([o_p], gate1, y0_p, w_out1_b, g_final, True, tm_p)

    k1s_3, v1s_3 = k1s_r.reshape(bs, nq, w_c), v1s_r.reshape(bs, nq, w_c)
    q1s_4 = q1s.reshape(bs, nq, heads, dh).transpose(0, 2, 1, 3)
    first = jnp.arange(dh) < dh // 2
    q1s_split = jnp.concatenate([jnp.where(first, q1s_4, 0.0), jnp.where(first, 0.0, q1s_4)], axis=2)
    qbd1_b = _block_diag_queries(q1s_split, heads, 2 * nq).astype(BF16)
    m1, o1 = _paged_partials(qbd1_b, None, caches[2], caches[3], page_table, heads, 2 * nq, (dh // 2) ** -0.5,
                             pages_per_step, "diff_sample_partials")
    o_s = _paged_combine(qbd1_b, k1s_3, v1s_3, m1, o1, heads, 2 * nq, (dh // 2) ** -0.5, (lams, g_subln),
                         "diff_sample_combine")
    y_s = _merge_out([o_s.reshape(bs * nq, w_c)], gate1s, y0_s, w_out1_b, g_final, True, tm_s)

    hd = (heads, dh)
    return (y_p.reshape(1, t, d), y_s.reshape(bs, nq, d),
            k0.reshape(1, t, *hd), v0.reshape(1, t, *hd), pool_p,
            k1.reshape(1, t, *hd), v1.reshape(1, t, *hd),
            k0s.reshape(bs, nq, *hd), v0s.reshape(bs, nq, *hd), pool_s,
            k1s.reshape(bs, nq, *hd), v1s.reshape(bs, nq, *hd))
```
